```python
import jax, jax.numpy as jnp
from jax import lax
import numpy as np

D_MODEL = 4096
BATCH = 8
SEQ = 2048
DEPTH = 4
DEC_BATCH = 8
DEC_SEQ = 64
PAST_LEN = 1024

CHUNK = 64
Q_BLOCK = 128
N_MIXERS = 3
N_A = (DEPTH + 2) // 3
N_B = (DEPTH + 1) // 3
N_C = DEPTH // 3
D_FF = 11008
EPS = 1e-6
NEG_INF = -1e30
MLA_HEADS = D_MODEL // 128
Q_LORA = D_MODEL // 4
KV_LORA = 512
NOPE_DIM = 128
ROPE_DIM = 64
V_DIM = 128
ROPE_THETA = 10000.0
MLA_SCALE = (NOPE_DIM + ROPE_DIM) ** -0.5
BAND_HEADS = D_MODEL // 128
BAND_HEAD_DIM = 128
LEFT_CHUNKS = 8
LEFT_CTX = LEFT_CHUNKS * CHUNK
REL_CLIP = 128
BAND_SCALE = BAND_HEAD_DIM ** -0.5
SB_HEADS = D_MODEL // 128
SB_HEAD_DIM = 128
SB_SCALE = SB_HEAD_DIM ** -0.5

kernel_name = 'hybrid_streaming_encoder_step'


def rms_norm(x, g):
    xf = x.astype(jnp.float32)
    y = xf * lax.rsqrt(jnp.mean(xf * xf, axis=-1, keepdims=True) + EPS)
    return (y * g.astype(jnp.float32)).astype(x.dtype)


def macaron_half(x, g, w_gu, w_down):
    gate, up = jnp.split(rms_norm(x, g) @ w_gu, 2, axis=-1)
    return x + 0.5 * ((jax.nn.silu(gate) * up) @ w_down)


def rope(x, pos):
    half = ROPE_DIM // 2
    inv_freq = ROPE_THETA ** (-jnp.arange(half, dtype=jnp.float32) / half)
    ang = pos.astype(jnp.float32)[:, None] * inv_freq[None, :]
    ang = ang.reshape((ang.shape[0],) + (1,) * (x.ndim - 3) + (half,))
    cos = jnp.cos(ang).astype(x.dtype)
    sin = jnp.sin(ang).astype(x.dtype)
    x1, x2 = x[..., :half], x[..., half:]
    return jnp.concatenate([x1 * cos - x2 * sin, x1 * sin + x2 * cos], axis=-1)


def mla_expand(c_kv, w_uk, w_uv):
    return (jnp.einsum('bsc,chn->bshn', c_kv, w_uk), jnp.einsum('bsc,chv->bshv', c_kv, w_uv))


def mla_attend(q_nope, q_rope, k_nope, k_rope, v, mask):
    s = jnp.einsum('bqhn,bshn->bhqs', q_nope, k_nope) + jnp.einsum('bqhr,bsr->bhqs', q_rope, k_rope)
    s = jnp.where(mask, s.astype(jnp.float32) * MLA_SCALE, NEG_INF)
    p = jax.nn.softmax(s, axis=-1).astype(v.dtype)
    return jnp.einsum('bhqs,bshv->bqhv', p, v)


def mla_mixer(h, pos, past_ckv, past_krope, w_down, g_q, w_uq, g_kv, w_uk, w_uv, w_o):
    B, T, _ = h.shape
    c_q, c_kv, k_rope = jnp.split(h @ w_down, [Q_LORA, Q_LORA + KV_LORA], axis=-1)
    q = (rms_norm(c_q, g_q) @ w_uq).reshape(B, T, MLA_HEADS, NOPE_DIM + ROPE_DIM)
    q_nope, q_rope = q[..., :NOPE_DIM], rope(q[..., NOPE_DIM:], pos)
    c_kv = rms_norm(c_kv, g_kv)
    k_rope = rope(k_rope, pos)
    if past_ckv is None:
        k_nope, v = mla_expand(c_kv, w_uk, w_uv)
        nb = T // Q_BLOCK
        key_chunk = jnp.arange(T) // CHUNK
        qn = q_nope.reshape(B, nb, Q_BLOCK, MLA_HEADS, NOPE_DIM).swapaxes(0, 1)
        qr = q_rope.reshape(B, nb, Q_BLOCK, MLA_HEADS, ROPE_DIM).swapaxes(0, 1)

        def block(args):
            b, qn_b, qr_b = args
            q_chunk = (b * Q_BLOCK + jnp.arange(Q_BLOCK)) // CHUNK
            mask = key_chunk[None, :] <= q_chunk[:, None]
            return mla_attend(qn_b, qr_b, k_nope, k_rope, v, mask)

        out = lax.map(block, (jnp.arange(nb), qn, qr)).swapaxes(0, 1)
    else:
        all_ckv = jnp.concatenate([past_ckv, c_kv], axis=1)
        all_krope = jnp.concatenate([past_krope, k_rope], axis=1)
        k_nope, v = mla_expand(all_ckv, w_uk, w_uv)
        mask = jnp.ones((T, all_ckv.shape[1]), dtype=bool)
        out = mla_attend(q_nope, q_rope, k_nope, all_krope, v, mask)
    return out.reshape(B, T, MLA_HEADS * V_DIM) @ w_o, c_kv, k_rope


def rel_bias_lookup(rel_bias, rel):
    return rel_bias[:, jnp.clip(rel, -REL_CLIP, REL_CLIP) + REL_CLIP].astype(jnp.float32)


def band_attend(q, kv, bias, mask):
    s = jnp.einsum('bqhd,bshd->bhqs', q, kv[:, :, 0]).astype(jnp.float32) * BAND_SCALE + bias[None]
    p = jax.nn.softmax(jnp.where(mask, s, NEG_INF), axis=-1).astype(kv.dtype)
    return jnp.einsum('bhqs,bshd->bqhd', p, kv[:, :, 1])


def band_mixer(h, past_kv, w_qkv, rel_bias, w_o):
    B, T, _ = h.shape
    qkv = (h @ w_qkv).reshape(B, T, 3, BAND_HEADS, BAND_HEAD_DIM)
    q, kv = qkv[:, :, 0], qkv[:, :, 1:]
    if past_kv is None:
        nc = T // CHUNK
        band = LEFT_CTX + CHUNK
        kv_pad = jnp.pad(kv, ((0, 0), (LEFT_CTX, 0), (0, 0), (0, 0), (0, 0)))
        rel = (LEFT_CTX + jnp.arange(CHUNK))[:, None] - jnp.arange(band)[None, :]
        bias = rel_bias_lookup(rel_bias, rel)
        qc = q.reshape(B, nc, CHUNK, BAND_HEADS, BAND_HEAD_DIM).swapaxes(0, 1)

        def one_chunk(args):
            c, q_c = args
            kv_b = lax.dynamic_slice_in_dim(kv_pad, c * CHUNK, band, axis=1)
            valid = (c * CHUNK - LEFT_CTX + jnp.arange(band)) >= 0
            mask = jnp.broadcast_to(valid[None, :], (CHUNK, band))
            return band_attend(q_c, kv_b, bias, mask)

        out = lax.map(one_chunk, (jnp.arange(nc), qc)).swapaxes(0, 1)
        win = min(LEFT_CTX, T)
        new_state = kv[:, T - win:]
    else:
        win = past_kv.shape[1]
        all_kv = jnp.concatenate([past_kv, kv], axis=1)
        rel = (win + jnp.arange(T))[:, None] - jnp.arange(win + T)[None, :]
        mask = jnp.ones((T, win + T), dtype=bool)
        out = band_attend(q, all_kv, rel_bias_lookup(rel_bias, rel), mask)
        new_state = all_kv[:, T:]
    return out.reshape(B, T, BAND_HEADS * BAND_HEAD_DIM) @ w_o, new_state


def sb_attend(q, kv, mask):
    z = jnp.einsum('bqhd,bshd->bhqs', q, kv[:, :, 0]).astype(jnp.float32) * SB_SCALE
    log_fail = jnp.where(mask, jax.nn.log_sigmoid(-z), 0.0)
    suffix = lax.cumsum(log_fail, axis=3, reverse=True) - log_fail
    a = jnp.where(mask, jnp.exp(jax.nn.log_sigmoid(z) + suffix), 0.0)
    return jnp.einsum('bhqs,bshd->bqhd', a.astype(kv.dtype), kv[:, :, 1])


def sb_mixer(h, past_kv, w_qkv, w_o):
    B, T, _ = h.shape
    qkv = (h @ w_qkv).reshape(B, T, 3, SB_HEADS, SB_HEAD_DIM)
    q, kv = qkv[:, :, 0], qkv[:, :, 1:]
    if past_kv is None:
        nb = T // Q_BLOCK
        key_pos = jnp.arange(T)
        qb = q.reshape(B, nb, Q_BLOCK, SB_HEADS, SB_HEAD_DIM).swapaxes(0, 1)

        def block(args):
            b, q_b = args
            q_pos = b * Q_BLOCK + jnp.arange(Q_BLOCK)
            return sb_attend(q_b, kv, key_pos[None, :] < q_pos[:, None])

        out = lax.map(block, (jnp.arange(nb), qb)).swapaxes(0, 1)
    else:
        past = past_kv.shape[1]
        all_kv = jnp.concatenate([past_kv, kv], axis=1)
        mask = jnp.arange(past + T)[None, :] < (past + jnp.arange(T))[:, None]
        out = sb_attend(q, all_kv, mask)
    return out.reshape(B, T, SB_HEADS * SB_HEAD_DIM) @ w_o, kv


def setup_inputs(seed: int = 0) -> dict:
    key = jax.random.key(seed)
    ks = iter(jax.random.split(key, 32))

    def nrm(shape, scale=1.0):
        return jax.random.normal(next(ks), shape, jnp.float32) * scale

    def gain(shape):
        return 1.0 + 0.1 * jax.random.normal(next(ks), shape, jnp.float32)

    band_win = min(LEFT_CTX, PAST_LEN)
    return {
        'x_prompt': nrm((BATCH, SEQ, D_MODEL)),
        'x_sample': nrm((DEC_BATCH, DEC_SEQ, D_MODEL)),
        'cache_mla_ckv': nrm((N_A, DEC_BATCH, PAST_LEN, KV_LORA)),
        'cache_mla_krope': nrm((N_A, DEC_BATCH, PAST_LEN, ROPE_DIM)),
        'cache_band_kv': nrm((N_B, DEC_BATCH, band_win, 2, BAND_HEADS, BAND_HEAD_DIM)),
        'cache_sb_kv': nrm((N_C, DEC_BATCH, PAST_LEN, 2, SB_HEADS, SB_HEAD_DIM)),
        'norm_g': gain((DEPTH, 3, D_MODEL)),
        'final_norm_g': gain((D_MODEL,)),
        'ffn_w_gu': nrm((DEPTH, 2, D_MODEL, 2 * D_FF), D_MODEL ** -0.5),
        'ffn_w_down': nrm((DEPTH, 2, D_FF, D_MODEL), D_FF ** -0.5),
        'mla_w_down': nrm((N_A, D_MODEL, Q_LORA + KV_LORA + ROPE_DIM), D_MODEL ** -0.5),
        'mla_q_norm_g': gain((N_A, Q_LORA)),
        'mla_kv_norm_g': gain((N_A, KV_LORA)),
        'mla_w_uq': nrm((N_A, Q_LORA, MLA_HEADS * (NOPE_DIM + ROPE_DIM)), Q_LORA ** -0.5),
        'mla_w_uk': nrm((N_A, KV_LORA, MLA_HEADS, NOPE_DIM), KV_LORA ** -0.5),
        'mla_w_uv': nrm((N_A, KV_LORA, MLA_HEADS, V_DIM), KV_LORA ** -0.5),
        'mla_w_o': nrm((N_A, MLA_HEADS * V_DIM, D_MODEL), (MLA_HEADS * V_DIM) ** -0.5),
        'band_w_qkv': nrm((N_B, D_MODEL, 3 * BAND_HEADS * BAND_HEAD_DIM), D_MODEL ** -0.5),
        'band_rel_bias': nrm((N_B, BAND_HEADS, 2 * REL_CLIP + 1), 0.5),
        'band_w_o': nrm((N_B, BAND_HEADS * BAND_HEAD_DIM, D_MODEL), (BAND_HEADS * BAND_HEAD_DIM) ** -0.5),
        'sb_w_qkv': nrm((N_C, D_MODEL, 3 * SB_HEADS * SB_HEAD_DIM), D_MODEL ** -0.5),
        'sb_w_o': nrm((N_C, SB_HEADS * SB_HEAD_DIM, D_MODEL), (SB_HEADS * SB_HEAD_DIM) ** -0.5),
    }


def reference(x_prompt, x_sample, cache_mla_ckv, cache_mla_krope, cache_band_kv, cache_sb_kv,
              norm_g, final_norm_g, ffn_w_gu, ffn_w_down,
              mla_w_down, mla_q_norm_g, mla_kv_norm_g, mla_w_uq, mla_w_uk, mla_w_uv, mla_w_o,
              band_w_qkv, band_rel_bias, band_w_o, sb_w_qkv, sb_w_o):
    xp, xs = x_prompt, x_sample
    past = cache_mla_ckv.shape[2]
    pos_p = jnp.arange(xp.shape[1])
    pos_s = past + jnp.arange(xs.shape[1])
    ckv_p, krope_p, band_p, sb_p = [], [], [], []
    ckv_s, krope_s, band_s, sb_s = [], [], [], []
    for i in range(DEPTH):
        kind, j = i % N_MIXERS, i // N_MIXERS
        xp = macaron_half(xp, norm_g[i, 0], ffn_w_gu[i, 0], ffn_w_down[i, 0])
        xs = macaron_half(xs, norm_g[i, 0], ffn_w_gu[i, 0], ffn_w_down[i, 0])
        hp = rms_norm(xp, norm_g[i, 1])
        hs = rms_norm(xs, norm_g[i, 1])
        if kind == 0:
            wts = (mla_w_down[j], mla_q_norm_g[j], mla_w_uq[j], mla_kv_norm_g[j], mla_w_uk[j], mla_w_uv[j], mla_w_o[j])
            mp, c_p, k_p = mla_mixer(hp, pos_p, None, None, *wts)
            ms, c_s, k_s = mla_mixer(hs, pos_s, cache_mla_ckv[j], cache_mla_krope[j], *wts)
            ckv_p.append(c_p); krope_p.append(k_p); ckv_s.append(c_s); krope_s.append(k_s)
        elif kind == 1:
            mp, st_p = band_mixer(hp, None, band_w_qkv[j], band_rel_bias[j], band_w_o[j])
            ms, st_s = band_mixer(hs, cache_band_kv[j], band_w_qkv[j], band_rel_bias[j], band_w_o[j])
            band_p.append(st_p); band_s.append(st_s)
        else:
            mp, st_p = sb_mixer(hp, None, sb_w_qkv[j], sb_w_o[j])
            ms, st_s = sb_mixer(hs, cache_sb_kv[j], sb_w_qkv[j], sb_w_o[j])
            sb_p.append(st_p); sb_s.append(st_s)
        xp = xp + mp
        xs = xs + ms
        xp = macaron_half(xp, norm_g[i, 2], ffn_w_gu[i, 1], ffn_w_down[i, 1])
        xs = macaron_half(xs, norm_g[i, 2], ffn_w_gu[i, 1], ffn_w_down[i, 1])
    y_prompt = rms_norm(xp, final_norm_g)
    y_sample = rms_norm(xs, final_norm_g)
    return (y_prompt, y_sample,
            jnp.stack(ckv_p), jnp.stack(krope_p), jnp.stack(band_p), jnp.stack(sb_p),
            jnp.stack(ckv_s), jnp.stack(krope_s), jnp.stack(band_s), jnp.stack(sb_s))
```

```python
import functools

import jax
import jax.numpy as jnp
import numpy as np
from jax import lax
from jax.experimental import pallas as pl
from jax.experimental.pallas import tpu as pltpu

F32 = jnp.float32
BF16 = jnp.bfloat16

EPS = 1e-6
NEG_INF = -1e30
CHUNK = 64
HEAD_DIM = 128
ROPE_DIM = 64
ROPE_THETA = 10000.0
LEFT_CHUNKS = 8
LEFT_CTX = LEFT_CHUNKS * CHUNK
REL_CLIP = 128
LANES = 128
MXU_DIM = 256
VMEM_LIMIT = 56 * 1024 * 1024
FFN_VMEM_LIMIT = 60 * 1024 * 1024


def _pick(n, candidates):
    for c in candidates:
        if n % c == 0:
            return c
    raise ValueError(f"no tile in {candidates} divides {n}")


def _params(*sem, vmem=VMEM_LIMIT):
    return pltpu.CompilerParams(dimension_semantics=sem, vmem_limit_bytes=vmem)


def _rms(x, g):
    return x * lax.rsqrt(jnp.mean(x * x, axis=-1, keepdims=True) + EPS) * g


def _dot(a, b):
    return jnp.dot(a, b, preferred_element_type=F32)


def _dot_nt(a, b):
    return lax.dot_general(a, b, (((1,), (1,)), ((), ())), preferred_element_type=F32)


def _ffn_kernel(x_ref, g_ref, wg_ref, wu_ref, wd_ref, o_ref, h_ref, *, n_col):
    @pl.when(pl.program_id(1) == 0)
    def _():
        x = x_ref[...]
        h_ref[...] = _rms(x, g_ref[...]).astype(BF16)
        o_ref[...] = x

    h = h_ref[...]
    gate = _dot(h, wg_ref[...])
    up = _dot(h, wu_ref[...])
    a = (gate * jax.nn.sigmoid(gate) * up).astype(BF16)
    d = o_ref.shape[1]
    for n in range(0, d, n_col):
        o_ref[:, n:n + n_col] += 0.5 * _dot(a, wd_ref[:, n:n + n_col])


def _ffn(x, g, w_gu, w_down):
    n, d = x.shape
    d_ff = w_down.shape[0]
    tm = _pick(n, (512, 256, 128))
    tf = _pick(d_ff, (256, 128))
    nf = d_ff // tf
    return pl.pallas_call(
        functools.partial(_ffn_kernel, n_col=_pick(d, (512, 256, 128))),
        grid=(n // tm, nf),
        in_specs=[
            pl.BlockSpec((tm, d), lambda i, j: (i, 0)),
            pl.BlockSpec((1, d), lambda i, j: (0, 0)),
            pl.BlockSpec((d, tf), lambda i, j: (0, j)),
            pl.BlockSpec((d, tf), lambda i, j: (0, j + nf)),
            pl.BlockSpec((tf, d), lambda i, j: (j, 0)),
        ],
        out_specs=pl.BlockSpec((tm, d), lambda i, j: (i, 0)),
        out_shape=jax.ShapeDtypeStruct((n, d), F32),
        scratch_shapes=[pltpu.VMEM((tm, d), BF16)],
        compiler_params=_params("parallel", "arbitrary", vmem=FFN_VMEM_LIMIT),
        name="ffn",
    )(x, g.reshape(1, d), w_gu, w_gu, w_down)


def _norm_mm_kernel(x_ref, g_ref, w_ref, o_ref, h_ref):
    @pl.when(pl.program_id(1) == 0)
    def _():
        h_ref[...] = _rms(x_ref[...], g_ref[...]).astype(BF16)

    o_ref[...] = _dot(h_ref[...], w_ref[...]).astype(o_ref.dtype)


def _norm_mm(x, g, w, out_dtype):
    n, d = x.shape
    nout = w.shape[1]
    tm = _pick(n, (512, 256, 128))
    tn = _pick(nout, (1024, 512, 256, 128))
    return pl.pallas_call(
        _norm_mm_kernel,
        grid=(n // tm, nout // tn),
        in_specs=[
            pl.BlockSpec((tm, d), lambda i, j: (i, 0)),
            pl.BlockSpec((1, d), lambda i, j: (0, 0)),
            pl.BlockSpec((d, tn), lambda i, j: (0, j)),
        ],
        out_specs=pl.BlockSpec((tm, tn), lambda i, j: (i, j)),
        out_shape=jax.ShapeDtypeStruct((n, nout), out_dtype),
        scratch_shapes=[pltpu.VMEM((tm, d), BF16)],
        compiler_params=_params("parallel", "arbitrary"),
        name="norm_mm",
    )(x, g.reshape(1, d), w)


def _mm_kernel(a_ref, w_ref, o_ref):
    o_ref[...] = _dot(a_ref[...], w_ref[...]).astype(o_ref.dtype)


def _mm(a, w, out_dtype):
    n, k = a.shape
    nout = w.shape[1]
    tm = _pick(n, (512, 256, 128, 64))
    tn = _pick(nout, (2048, 1024, 512, 256, 128))
    return pl.pallas_call(
        _mm_kernel,
        grid=(n // tm, nout // tn),
        in_specs=[
            pl.BlockSpec((tm, k), lambda i, j: (i, 0)),
            pl.BlockSpec((k, tn), lambda i, j: (0, j)),
        ],
        out_specs=pl.BlockSpec((tm, tn), lambda i, j: (i, j)),
        out_shape=jax.ShapeDtypeStruct((n, nout), out_dtype),
        compiler_params=_params("parallel", "parallel"),
        name="mm",
    )(a, w)


def _mm_res_kernel(a_ref, w_ref, r_ref, o_ref):
    o_ref[...] = r_ref[...] + _dot(a_ref[...], w_ref[...])


def _mm_res(a, w, res):
    n, k = a.shape
    nout = w.shape[1]
    tm = _pick(n, (512, 256, 128))
    tn = _pick(nout, (1024, 512, 256, 128))
    return pl.pallas_call(
        _mm_res_kernel,
        grid=(n // tm, nout // tn),
        in_specs=[
            pl.BlockSpec((tm, k), lambda i, j: (i, 0)),
            pl.BlockSpec((k, tn), lambda i, j: (0, j)),
            pl.BlockSpec((tm, tn), lambda i, j: (i, j)),
        ],
        out_specs=pl.BlockSpec((tm, tn), lambda i, j: (i, j)),
        out_shape=jax.ShapeDtypeStruct((n, nout), F32),
        compiler_params=_params("parallel", "parallel"),
        name="mm_res",
    )(a, w, res)


def _final_norm_kernel(x_ref, g_ref, o_ref):
    o_ref[...] = _rms(x_ref[...], g_ref[...])


def _final_norm(x, g, row0, rows):
    d = x.shape[1]
    tm = _pick(rows, (512, 256, 128))
    assert row0 % tm == 0
    off = row0 // tm
    return pl.pallas_call(
        _final_norm_kernel,
        grid=(rows // tm,),
        in_specs=[
            pl.BlockSpec((tm, d), lambda i: (i + off, 0)),
            pl.BlockSpec((1, d), lambda i: (0, 0)),
        ],
        out_specs=pl.BlockSpec((tm, d), lambda i: (i, 0)),
        out_shape=jax.ShapeDtypeStruct((rows, d), F32),
        compiler_params=_params("parallel"),
        name="final_norm",
    )(x, g.reshape(1, d))


def _rope_group(r, cosz, sinz):
    return r * cosz + pltpu.roll(r, ROPE_DIM, axis=1) * sinz


def _mla_down_kernel(x_ref, g_ref, w_ref, gq_ref, gkv_ref, cosz_ref, sinz_ref,
                     qn_ref, ckv_ref, ckvb_ref, kr_ref, krb_ref, *, q_lora, kv_lora):
    h = _rms(x_ref[...], g_ref[...]).astype(BF16)
    y = _dot(h, w_ref[...])
    qn_ref[...] = _rms(y[:, :q_lora], gq_ref[...]).astype(BF16)
    ckv = _rms(y[:, q_lora:q_lora + kv_lora], gkv_ref[...])
    ckv_ref[...] = ckv
    ckvb_ref[...] = ckv.astype(BF16)
    r0 = q_lora + kv_lora
    kr = _rope_group(y[:, r0:r0 + 2 * ROPE_DIM], cosz_ref[...], sinz_ref[...])
    kr_ref[...] = kr[:, :ROPE_DIM]
    krb_ref[...] = kr.astype(BF16)


def _mla_down(x, g, w_ext, gq, gkv, cosz, sinz, q_lora, kv_lora):
    n, d = x.shape
    wn = w_ext.shape[1]
    tm = _pick(n, (256, 128))
    row = lambda i: (i, 0)
    fix = lambda i: (0, 0)
    return pl.pallas_call(
        functools.partial(_mla_down_kernel, q_lora=q_lora, kv_lora=kv_lora),
        grid=(n // tm,),
        in_specs=[
            pl.BlockSpec((tm, d), row),
            pl.BlockSpec((1, d), fix),
            pl.BlockSpec((d, wn), fix),
            pl.BlockSpec((1, q_lora), fix),
            pl.BlockSpec((1, kv_lora), fix),
            pl.BlockSpec((tm, LANES), row),
            pl.BlockSpec((tm, LANES), row),
        ],
        out_specs=[
            pl.BlockSpec((tm, q_lora), row),
            pl.BlockSpec((tm, kv_lora), row),
            pl.BlockSpec((tm, kv_lora), row),
            pl.BlockSpec((tm, ROPE_DIM), row),
            pl.BlockSpec((tm, 2 * ROPE_DIM), row),
        ],
        out_shape=[
            jax.ShapeDtypeStruct((n, q_lora), BF16),
            jax.ShapeDtypeStruct((n, kv_lora), F32),
            jax.ShapeDtypeStruct((n, kv_lora), BF16),
            jax.ShapeDtypeStruct((n, ROPE_DIM), F32),
            jax.ShapeDtypeStruct((n, 2 * ROPE_DIM), BF16),
        ],
        compiler_params=_params("parallel"),
        name="mla_down",
    )(x, g.reshape(1, d), w_ext, gq.reshape(1, -1), gkv.reshape(1, -1), cosz, sinz)


def _mla_q_kernel(a_ref, w_ref, cosz_ref, sinz_ref, o_ref):
    y = _dot(a_ref[...], w_ref[...])
    cosz = cosz_ref[...]
    sinz = sinz_ref[...]
    for c in range(0, y.shape[1], 2 * HEAD_DIM):
        rot = _rope_group(y[:, c + HEAD_DIM:c + 2 * HEAD_DIM], cosz, sinz)
        o_ref[:, c:c + HEAD_DIM] = y[:, c:c + HEAD_DIM].astype(BF16)
        o_ref[:, c + HEAD_DIM:c + 2 * HEAD_DIM] = rot.astype(BF16)


def _mla_q(qn, w_ext, cosz, sinz):
    n, k = qn.shape
    nout = w_ext.shape[1]
    tm = _pick(n, (512, 256, 128))
    tn = _pick(nout, (2048, 1024, 512, 256))
    return pl.pallas_call(
        _mla_q_kernel,
        grid=(n // tm, nout // tn),
        in_specs=[
            pl.BlockSpec((tm, k), lambda i, j: (i, 0)),
            pl.BlockSpec((k, tn), lambda i, j: (0, j)),
            pl.BlockSpec((tm, LANES), lambda i, j: (i, 0)),
            pl.BlockSpec((tm, LANES), lambda i, j: (i, 0)),
        ],
        out_specs=pl.BlockSpec((tm, tn), lambda i, j: (i, j)),
        out_shape=jax.ShapeDtypeStruct((n, nout), BF16),
        compiler_params=_params("parallel", "parallel"),
        name="mla_q",
    )(qn, w_ext, cosz, sinz)


def _softmax_pv(parts, v_ref, k0):
    m = parts[0].max(axis=-1, keepdims=True)
    for s in parts[1:]:
        m = jnp.maximum(m, s.max(axis=-1, keepdims=True))
    l = None
    acc = None
    for s in parts:
        w = s.shape[1]
        p = jnp.exp(s - m)
        ps = p.sum(axis=-1, keepdims=True)
        pv = _dot(p.astype(BF16), v_ref[k0:k0 + w, :])
        l = ps if l is None else l + ps
        acc = pv if acc is None else acc + pv
        k0 += w
    return acc / l


def _mla_attn_kernel(q_ref, kn_ref, kr_ref, v_ref, o_ref, *, tq, n_past, scale):
    t = q_ref.shape[0]
    if n_past is None:
        r = lax.broadcasted_iota(jnp.int32, (tq, tq), 0)
        c = lax.broadcasted_iota(jnp.int32, (tq, tq), 1)
        diag = c < (r | (CHUNK - 1)) + 1
    for i in range(t // tq):
        q = q_ref[i * tq:(i + 1) * tq, :]
        n_keys = (i + 1) * tq if n_past is None else n_past + t
        k = jnp.concatenate([kn_ref[0:n_keys, :], kr_ref[0:n_keys, :]], axis=1)
        s = _dot_nt(q, k) * scale
        if n_past is None:
            parts = [s[:, :n_keys - tq]] if i else []
            parts.append(jnp.where(diag, s[:, n_keys - tq:], NEG_INF))
        else:
            parts = [s]
        o_ref[i * tq:(i + 1) * tq, :] = _softmax_pv(parts, v_ref, 0).astype(o_ref.dtype)


def _mla_attn(q, kv, kr, batch, t, s_len, heads, q_row0, n_past, scale):
    tq = t if n_past is not None else _pick(t, (256, 128))
    assert q_row0 % t == 0
    qoff = q_row0 // t
    return pl.pallas_call(
        functools.partial(_mla_attn_kernel, tq=tq, n_past=n_past, scale=scale),
        grid=(batch, heads),
        in_specs=[
            pl.BlockSpec((t, 2 * HEAD_DIM), lambda b, h: (b + qoff, h)),
            pl.BlockSpec((s_len, HEAD_DIM), lambda b, h: (b, h)),
            pl.BlockSpec((s_len, HEAD_DIM), lambda b, h: (b, 0)),
            pl.BlockSpec((s_len, HEAD_DIM), lambda b, h: (b, h + heads)),
        ],
        out_specs=pl.BlockSpec((t, HEAD_DIM), lambda b, h: (b, h)),
        out_shape=jax.ShapeDtypeStruct((batch * t, heads * HEAD_DIM), BF16),
        compiler_params=_params("parallel", "parallel"),
        name="mla_attn",
    )(q, kv, kr, kv)


def _band_attn_kernel(q_ref, k_ref, v_ref, bias_ref, o_ref, *, tq, scale):
    t = q_ref.shape[0]
    s_len = k_ref.shape[0]
    band = bias_ref.shape[2]
    lead = s_len - t
    for i in range(t // tq):
        ke = lead + (i + 1) * tq
        ks = max(0, ke - band)
        q = q_ref[i * tq:(i + 1) * tq, :]
        s = _dot_nt(q, k_ref[ks:ke, :]) * scale + bias_ref[0, :, band - (ke - ks):]
        o_ref[i * tq:(i + 1) * tq, :] = _softmax_pv([s], v_ref, ks).astype(o_ref.dtype)


def _band_attn(qkv, q_col0, kv, kv_col0, bias, batch, t, s_len, heads, q_row0, scale):
    tq = bias.shape[1]
    assert q_row0 % t == 0
    qoff = q_row0 // t
    return pl.pallas_call(
        functools.partial(_band_attn_kernel, tq=tq, scale=scale),
        grid=(batch, heads),
        in_specs=[
            pl.BlockSpec((t, HEAD_DIM), lambda b, h: (b + qoff, h + q_col0)),
            pl.BlockSpec((s_len, HEAD_DIM), lambda b, h: (b, h + kv_col0)),
            pl.BlockSpec((s_len, HEAD_DIM), lambda b, h: (b, h + kv_col0 + heads)),
            pl.BlockSpec((1,) + bias.shape[1:], lambda b, h: (h, 0, 0)),
        ],
        out_specs=pl.BlockSpec((t, HEAD_DIM), lambda b, h: (b, h)),
        out_shape=jax.ShapeDtypeStruct((batch * t, heads * HEAD_DIM), BF16),
        compiler_params=_params("parallel", "parallel"),
        name="band_attn",
    )(qkv, kv, kv, bias)


def _softplus(z):
    return jnp.maximum(z, 0.0) + jnp.log1p(jnp.exp(-jnp.abs(z)))


def _sb_attn_kernel(q_ref, k_ref, v_ref, o_ref, *, tq, tk, scale):
    t = q_ref.shape[0]
    s_len = k_ref.shape[0]
    lead = s_len - t
    tri_n = max(tq, tk)
    tri = jnp.where(lax.broadcasted_iota(jnp.int32, (tri_n, tri_n), 0)
                    > lax.broadcasted_iota(jnp.int32, (tri_n, tri_n), 1), 1.0, 0.0).astype(BF16)
    for i in range(t // tq):
        q = q_ref[i * tq:(i + 1) * tq, :]
        blocks = [(lead + i * tq, tq, True)]
        end = lead + i * tq
        while end > 0:
            w = min(tk, end)
            blocks.append((end - w, w, False))
            end -= w
        carry = jnp.zeros((tq, 1), F32)
        acc = jnp.zeros((tq, HEAD_DIM), F32)
        for (k0, w, is_diag) in blocks:
            z = _dot_nt(q, k_ref[k0:k0 + w, :]) * scale
            lf = -_softplus(z)
            if is_diag:
                keep = (lax.broadcasted_iota(jnp.int32, (tq, w), 1)
                        < lax.broadcasted_iota(jnp.int32, (tq, w), 0))
                lf = jnp.where(keep, lf, 0.0)
            hi = lf.astype(BF16)
            lo = (lf - hi.astype(F32)).astype(BF16)
            suffix = _dot(hi, tri[:w, :w]) + _dot(lo, tri[:w, :w]) + carry
            a = jnp.exp(z + lf + suffix)
            if is_diag:
                a = jnp.where(keep, a, 0.0)
            acc = acc + _dot(a.astype(BF16), v_ref[k0:k0 + w, :])
            carry = carry + lf.sum(axis=-1, keepdims=True)
        o_ref[i * tq:(i + 1) * tq, :] = acc.astype(o_ref.dtype)


def _sb_attn(qkv, q_col0, kv, kv_col0, batch, t, s_len, heads, q_row0, scale):
    tq = _pick(t, (256, 128, 64))
    assert q_row0 % t == 0
    qoff = q_row0 // t
    return pl.pallas_call(
        functools.partial(_sb_attn_kernel, tq=tq, tk=MXU_DIM, scale=scale),
        grid=(batch, heads),
        in_specs=[
            pl.BlockSpec((t, HEAD_DIM), lambda b, h: (b + qoff, h + q_col0)),
            pl.BlockSpec((s_len, HEAD_DIM), lambda b, h: (b, h + kv_col0)),
            pl.BlockSpec((s_len, HEAD_DIM), lambda b, h: (b, h + kv_col0 + heads)),
        ],
        out_specs=pl.BlockSpec((t, HEAD_DIM), lambda b, h: (b, h)),
        out_shape=jax.ShapeDtypeStruct((batch * t, heads * HEAD_DIM), BF16),
        compiler_params=_params("parallel", "parallel"),
        name="sb_attn",
    )(qkv, kv, kv)


def _rope_tables(pos):
    half = ROPE_DIM // 2
    inv_freq = ROPE_THETA ** (-jnp.arange(half, dtype=F32) / half)
    ang = pos.astype(F32)[:, None] * inv_freq[None, :]
    cos = jnp.cos(ang)
    sin = jnp.sin(ang)
    zero = jnp.zeros((pos.shape[0], LANES - ROPE_DIM), F32)
    return jnp.concatenate([cos, cos, zero], axis=1), jnp.concatenate([sin, sin, zero], axis=1)


def _rotate_cols(w):
    half = ROPE_DIM // 2
    return jnp.concatenate([-w[..., half:], w[..., :half]], axis=-1)


def _mla_layer(x, g, cosz, sinz, past_ckv, past_krope, w_down, g_q, w_uq, g_kv, w_uk, w_uv, w_o,
               dims):
    bp, tp, bs, ts = dims
    np_rows = bp * tp
    q_lora = g_q.shape[0]
    kv_lora = g_kv.shape[0]
    heads = w_uk.shape[1]
    past = past_ckv.shape[1]
    scale = (HEAD_DIM + ROPE_DIM) ** -0.5

    w_rope = w_down[:, q_lora + kv_lora:]
    w_down_ext = jnp.concatenate([w_down, _rotate_cols(w_rope)], axis=1).astype(BF16)
    wq = w_uq.reshape(q_lora, heads, HEAD_DIM + ROPE_DIM)
    wq_rope = wq[..., HEAD_DIM:]
    w_uq_ext = jnp.concatenate([wq, _rotate_cols(wq_rope)], axis=-1).reshape(q_lora, heads * 2 * HEAD_DIM).astype(BF16)
    w_ukv = jnp.concatenate([w_uk.reshape(kv_lora, -1), w_uv.reshape(kv_lora, -1)], axis=1).astype(BF16)

    qn, ckv, ckv_b, kr, kr_b = _mla_down(x, g, w_down_ext, g_q, g_kv, cosz, sinz, q_lora, kv_lora)
    q = _mla_q(qn, w_uq_ext, cosz, sinz)

    kv_p = _mm(ckv_b[:np_rows], w_ukv, BF16)
    o_p = _mla_attn(q, kv_p, kr_b[:np_rows], bp, tp, tp, heads, 0, None, scale)

    all_ckv = jnp.concatenate([past_ckv.astype(BF16), ckv_b[np_rows:].reshape(bs, ts, kv_lora)], axis=1)
    kr_pad = jnp.pad(past_krope, ((0, 0), (0, 0), (0, ROPE_DIM))).astype(BF16)
    all_kr = jnp.concatenate([kr_pad, kr_b[np_rows:].reshape(bs, ts, 2 * ROPE_DIM)], axis=1)
    s_len = past + ts
    kv_s = _mm(all_ckv.reshape(bs * s_len, kv_lora), w_ukv, BF16)
    o_s = _mla_attn(q, kv_s, all_kr.reshape(bs * s_len, 2 * ROPE_DIM), bs, ts, s_len, heads, np_rows, past, scale)

    x = _mm_res(jnp.concatenate([o_p, o_s], axis=0), w_o.astype(BF16), x)
    outs = (ckv[:np_rows].reshape(bp, tp, kv_lora), kr[:np_rows].reshape(bp, tp, ROPE_DIM),
            ckv[np_rows:].reshape(bs, ts, kv_lora), kr[np_rows:].reshape(bs, ts, ROPE_DIM))
    return x, outs


def _band_bias(rel_bias, tq):
    band = LEFT_CTX + CHUNK
    rel = (LEFT_CTX + jnp.arange(CHUNK))[:, None] - jnp.arange(band)[None, :]
    bias = rel_bias[:, jnp.clip(rel, -REL_CLIP, REL_CLIP) + REL_CLIP].astype(F32)
    nch = tq // CHUNK
    rows = [jnp.pad(bias, ((0, 0), (0, 0), (c * CHUNK, (nch - 1 - c) * CHUNK)), constant_values=NEG_INF)
            for c in range(nch)]
    return jnp.concatenate(rows, axis=1)


def _band_layer(x, g, past_kv, w_qkv, rel_bias, w_o, dims):
    bp, tp, bs, ts = dims
    np_rows = bp * tp
    heads = rel_bias.shape[0]
    hd = heads * HEAD_DIM
    win = past_kv.shape[1]
    scale = HEAD_DIM ** -0.5

    qkv = _norm_mm(x, g, w_qkv.astype(BF16), F32)
    qkv_b = qkv.astype(BF16)
    o_p = _band_attn(qkv_b, 0, qkv_b, heads, _band_bias(rel_bias, 2 * CHUNK), bp, tp, tp, heads, 0, scale)

    new_kv = qkv[np_rows:, hd:].reshape(bs, ts, 2 * hd)
    all_kv = jnp.concatenate([past_kv.reshape(bs, win, 2 * hd), new_kv], axis=1)
    s_len = win + ts
    o_s = _band_attn(qkv_b, 0, all_kv.astype(BF16).reshape(bs * s_len, 2 * hd), 0, _band_bias(rel_bias, CHUNK),
                     bs, ts, s_len, heads, np_rows, scale)

    x = _mm_res(jnp.concatenate([o_p, o_s], axis=0), w_o.astype(BF16), x)
    wp = min(LEFT_CTX, tp)
    st_p = qkv[:np_rows, hd:].reshape(bp, tp, 2, heads, HEAD_DIM)[:, tp - wp:]
    st_s = all_kv[:, ts:].reshape(bs, win, 2, heads, HEAD_DIM)
    return x, (st_p, st_s)


def _sb_layer(x, g, past_kv, w_qkv, w_o, dims):
    bp, tp, bs, ts = dims
    np_rows = bp * tp
    hd = w_o.shape[0]
    heads = hd // HEAD_DIM
    past = past_kv.shape[1]
    scale = HEAD_DIM ** -0.5

    qkv = _norm_mm(x, g, w_qkv.astype(BF16), F32)
    qkv_b = qkv.astype(BF16)
    o_p = _sb_attn(qkv_b, 0, qkv_b, heads, bp, tp, tp, heads, 0, scale)

    s_len = past + ts
    all_kv = jnp.concatenate([past_kv.reshape(bs, past, 2 * hd).astype(BF16),
                              qkv_b[np_rows:, hd:].reshape(bs, ts, 2 * hd)], axis=1)
    o_s = _sb_attn(qkv_b, 0, all_kv.reshape(bs * s_len, 2 * hd), 0, bs, ts, s_len, heads, np_rows, scale)

    x = _mm_res(jnp.concatenate([o_p, o_s], axis=0), w_o.astype(BF16), x)
    st_p = qkv[:np_rows, hd:].reshape(bp, tp, 2, heads, HEAD_DIM)
    st_s = qkv[np_rows:, hd:].reshape(bs, ts, 2, heads, HEAD_DIM)
    return x, (st_p, st_s)


def kernel(x_prompt, x_sample, cache_mla_ckv, cache_mla_krope, cache_band_kv, cache_sb_kv, norm_g, final_norm_g, ffn_w_gu, ffn_w_down, mla_w_down, mla_q_norm_g, mla_kv_norm_g, mla_w_uq, mla_w_uk, mla_w_uv, mla_w_o, band_w_qkv, band_rel_bias, band_w_o, sb_w_qkv, sb_w_o):
    bp, tp, d = x_prompt.shape
    bs, ts, _ = x_sample.shape
    depth = norm_g.shape[0]
    past = cache_mla_ckv.shape[2]
    dims = (bp, tp, bs, ts)
    np_rows = bp * tp
    ns_rows = bs * ts

    x = jnp.concatenate([x_prompt.reshape(np_rows, d), x_sample.reshape(ns_rows, d)], axis=0)

    pos = jnp.concatenate([jnp.tile(jnp.arange(tp), bp), jnp.tile(past + jnp.arange(ts), bs)])
    cosz, sinz = _rope_tables(pos)

    ckv_p, krope_p, band_p, sb_p = [], [], [], []
    ckv_s, krope_s, band_s, sb_s = [], [], [], []
    for i in range(depth):
        kind, j = i % 3, i // 3
        x = _ffn(x, norm_g[i, 0], ffn_w_gu[i, 0].astype(BF16), ffn_w_down[i, 0].astype(BF16))
        if kind == 0:
            x, (c_p, k_p, c_s, k_s) = _mla_layer(
                x, norm_g[i, 1], cosz, sinz, cache_mla_ckv[j], cache_mla_krope[j],
                mla_w_down[j], mla_q_norm_g[j], mla_w_uq[j], mla_kv_norm_g[j], mla_w_uk[j], mla_w_uv[j],
                mla_w_o[j], dims)
            ckv_p.append(c_p); krope_p.append(k_p); ckv_s.append(c_s); krope_s.append(k_s)
        elif kind == 1:
            x, (st_p, st_s) = _band_layer(x, norm_g[i, 1], cache_band_kv[j], band_w_qkv[j], band_rel_bias[j],
                                          band_w_o[j], dims)
            band_p.append(st_p); band_s.append(st_s)
        else:
            x, (st_p, st_s) = _sb_layer(x, norm_g[i, 1], cache_sb_kv[j], sb_w_qkv[j], sb_w_o[j], dims)
            sb_p.append(st_p); sb_s.append(st_s)
        x = _ffn(x, norm_g[i, 2], ffn_w_gu[i, 1].astype(BF16), ffn_w_down[i, 1].astype(BF16))

    y_prompt = _final_norm(x, final_norm_g, 0, np_rows).reshape(bp, tp, d)
    y_sample = _final_norm(x, final_norm_g, np_rows, ns_rows).reshape(bs, ts, d)
    return (y_prompt, y_sample,
            jnp.stack(ckv_p), jnp.stack(krope_p), jnp.stack(band_p), jnp.stack(sb_p),
            jnp.stack(ckv_s), jnp.stack(krope_s), jnp.stack(band_s), jnp.stack(sb_s))
```

```python
import functools
import math

import jax
import jax.numpy as jnp
from jax import lax
from jax.experimental import pallas as pl
from jax.experimental.pallas import tpu as pltpu

F32 = jnp.float32
BF16 = jnp.bfloat16

EPS = 1e-6
NEG_INF = -1e30
CHUNK = 64
HEAD_DIM = 128
ROPE_DIM = 64
ROPE_THETA = 10000.0
LEFT_CHUNKS = 8
LEFT_CTX = LEFT_CHUNKS * CHUNK
REL_CLIP = 128
LANES = 128
MXU_DIM = 256
LOG2E = math.log2(math.e)
VMEM_LIMIT = 56 * 1024 * 1024
FFN_VMEM_LIMIT = 60 * 1024 * 1024


def _pick(n, candidates):
    for c in candidates:
        if n % c == 0:
            return c
    raise ValueError(f"no tile in {candidates} divides {n}")


def _params(*sem, vmem=VMEM_LIMIT):
    return pltpu.CompilerParams(dimension_semantics=sem, vmem_limit_bytes=vmem)


def _rms(x, g):
    return x * lax.rsqrt(jnp.mean(x * x, axis=-1, keepdims=True) + EPS) * g


def _dot(a, b):
    return jnp.dot(a, b, preferred_element_type=F32)


def _dot_nt(a, b):
    return lax.dot_general(a, b, (((1,), (1,)), ((), ())), preferred_element_type=F32)


def _skewed(items, stages):
    n, ns = len(items), len(stages)
    live = {}
    for step in range(n + ns - 1):
        for s in range(ns):
            t = step - s
            if 0 <= t < n:
                live[t] = stages[s](items[t], live.get(t))
        live.pop(step - ns + 1, None)


def _ffn_kernel(*refs, n_col, cast_next):
    x_ref, g_ref, wg_ref, wu_ref, wd_ref = refs[:5]
    if cast_next:
        ngu_ref, ndn_ref, o_ref, ngu_out, ndn_out, h_ref = refs[5:]
    else:
        o_ref, h_ref = refs[5:]

    @pl.when(pl.program_id(1) == 0)
    def _():
        x = x_ref[...]
        h_ref[...] = _rms(x, g_ref[...]).astype(BF16)
        o_ref[...] = x

    h = h_ref[...]
    gate = _dot(h, wg_ref[...])
    if cast_next:
        ngu_out[...] = ngu_ref[...].astype(BF16)
        ndn_out[...] = ndn_ref[...].astype(BF16)
    up = _dot(h, wu_ref[...])
    a = (gate * jax.nn.sigmoid(gate) * up).astype(BF16)
    d = o_ref.shape[1]
    for n in range(0, d, n_col):
        o_ref[:, n:n + n_col] += 0.5 * _dot(a, wd_ref[:, n:n + n_col])


def _ffn(x, g, w_gu, w_down, nxt=None):
    n, d = x.shape
    d_ff = w_down.shape[0]
    tm = _pick(n, (512, 256, 128))
    tf = _pick(d_ff, (256, 128))
    ni, nf = n // tm, d_ff // tf
    in_specs = [
        pl.BlockSpec((tm, d), lambda i, j: (i, 0)),
        pl.BlockSpec((1, d), lambda i, j: (0, 0)),
        pl.BlockSpec((d, tf), lambda i, j: (0, j)),
        pl.BlockSpec((d, tf), lambda i, j: (0, j + nf)),
        pl.BlockSpec((tf, d), lambda i, j: (j, 0)),
    ]
    out_specs = [pl.BlockSpec((tm, d), lambda i, j: (i, 0))]
    out_shape = [jax.ShapeDtypeStruct((n, d), F32)]
    args = [x, g.reshape(1, d), w_gu, w_gu, w_down]
    if nxt is not None:
        next_gu, next_dn, layer, half = nxt
        cr = min(1 << (ni.bit_length() - 1), d // LANES)
        rows, cols = d // cr, 2 * tf
        slab_i = lambda i: jnp.minimum(i, cr - 1)
        slab_j = lambda i, j: jnp.where(i < cr, j, nf - 1)
        in_specs += [
            pl.BlockSpec((None, None, rows, cols), lambda i, j: (layer, half, slab_i(i), slab_j(i, j))),
            pl.BlockSpec((None, None, tf, rows), lambda i, j: (layer, half, slab_j(i, j), slab_i(i))),
        ]
        out_specs += [
            pl.BlockSpec((rows, cols), lambda i, j: (slab_i(i), slab_j(i, j))),
            pl.BlockSpec((tf, rows), lambda i, j: (slab_j(i, j), slab_i(i))),
        ]
        out_shape += [jax.ShapeDtypeStruct((d, 2 * d_ff), BF16), jax.ShapeDtypeStruct((d_ff, d), BF16)]
        args += [next_gu, next_dn]
    out = pl.pallas_call(
        functools.partial(_ffn_kernel, n_col=_pick(d, (512, 256, 128)), cast_next=nxt is not None),
        grid=(ni, nf),
        in_specs=in_specs,
        out_specs=out_specs,
        out_shape=out_shape,
        scratch_shapes=[pltpu.VMEM((tm, d), BF16)],
        compiler_params=_params("arbitrary", "arbitrary", vmem=FFN_VMEM_LIMIT),
        name="ffn",
    )(*args)
    return out if nxt is not None else out[0]


def _qkv_kernel(x_ref, g_ref, w_ref, ob_ref, of_ref, h_ref, *, q_blocks):
    j = pl.program_id(1)

    @pl.when(j == 0)
    def _():
        h_ref[...] = _rms(x_ref[...], g_ref[...]).astype(BF16)

    y = _dot(h_ref[...], w_ref[...])
    ob_ref[...] = y.astype(BF16)

    @pl.when(j >= q_blocks)
    def _():
        of_ref[...] = y


def _qkv(x, g, w, q_cols):
    n, d = x.shape
    nout = w.shape[1]
    tm = _pick(n, (512, 256, 128))
    tn = _pick(math.gcd(q_cols, nout), (1024, 512, 256, 128))
    qb = q_cols // tn
    return pl.pallas_call(
        functools.partial(_qkv_kernel, q_blocks=qb),
        grid=(n // tm, nout // tn),
        in_specs=[
            pl.BlockSpec((tm, d), lambda i, j: (i, 0)),
            pl.BlockSpec((1, d), lambda i, j: (0, 0)),
            pl.BlockSpec((d, tn), lambda i, j: (0, j)),
        ],
        out_specs=[
            pl.BlockSpec((tm, tn), lambda i, j: (i, j)),
            pl.BlockSpec((tm, tn), lambda i, j: (i, jnp.maximum(j - qb, 0))),
        ],
        out_shape=[
            jax.ShapeDtypeStruct((n, nout), BF16),
            jax.ShapeDtypeStruct((n, nout - q_cols), F32),
        ],
        scratch_shapes=[pltpu.VMEM((tm, d), BF16)],
        compiler_params=_params("parallel", "arbitrary"),
        name="qkv",
    )(x, g.reshape(1, d), w)


def _mm_res_kernel(a_ref, w_ref, r_ref, o_ref):
    o_ref[...] = r_ref[...] + _dot(a_ref[...], w_ref[...])


def _mm_res(a, w, res):
    n, k = a.shape
    nout = w.shape[1]
    tm = _pick(n, (512, 256, 128))
    tn = _pick(nout, (1024, 512, 256, 128))
    return pl.pallas_call(
        _mm_res_kernel,
        grid=(n // tm, nout // tn),
        in_specs=[
            pl.BlockSpec((tm, k), lambda i, j: (i, 0)),
            pl.BlockSpec((k, tn), lambda i, j: (0, j)),
            pl.BlockSpec((tm, tn), lambda i, j: (i, j)),
        ],
        out_specs=pl.BlockSpec((tm, tn), lambda i, j: (i, j)),
        out_shape=jax.ShapeDtypeStruct((n, nout), F32),
        compiler_params=_params("parallel", "parallel"),
        name="mm_res",
    )(a, w, res)


def _final_norm_kernel(x_ref, g_ref, o_ref):
    o_ref[...] = _rms(x_ref[...], g_ref[...])


def _final_norm(x, g, row0, rows):
    d = x.shape[1]
    tm = _pick(rows, (512, 256, 128))
    assert row0 % tm == 0
    off = row0 // tm
    return pl.pallas_call(
        _final_norm_kernel,
        grid=(rows // tm,),
        in_specs=[
            pl.BlockSpec((tm, d), lambda i: (i + off, 0)),
            pl.BlockSpec((1, d), lambda i: (0, 0)),
        ],
        out_specs=pl.BlockSpec((tm, d), lambda i: (i, 0)),
        out_shape=jax.ShapeDtypeStruct((rows, d), F32),
        compiler_params=_params("parallel"),
        name="final_norm",
    )(x, g.reshape(1, d))


def _rope_group(r, cosz, sinz):
    return r * cosz + pltpu.roll(r, ROPE_DIM, axis=1) * sinz


def _mla_down_kernel(x_ref, g_ref, w_ref, gq_ref, gkv_ref, cosz_ref, sinz_ref,
                     qn_ref, ckv_ref, ckvb_ref, kr_ref, krb_ref, *, q_lora, kv_lora):
    h = _rms(x_ref[...], g_ref[...]).astype(BF16)
    y = _dot(h, w_ref[...])
    qn_ref[...] = _rms(y[:, :q_lora], gq_ref[...]).astype(BF16)
    ckv = _rms(y[:, q_lora:q_lora + kv_lora], gkv_ref[...])
    ckv_ref[...] = ckv
    ckvb_ref[...] = ckv.astype(BF16)
    r0 = q_lora + kv_lora
    kr = _rope_group(y[:, r0:r0 + 2 * ROPE_DIM], cosz_ref[...], sinz_ref[...])
    kr_ref[...] = kr[:, :ROPE_DIM]
    krb_ref[...] = kr.astype(BF16)


def _mla_down(x, g, w_ext, gq, gkv, cosz, sinz, q_lora, kv_lora):
    n, d = x.shape
    wn = w_ext.shape[1]
    tm = _pick(n, (256, 128))
    row = lambda i: (i, 0)
    fix = lambda i: (0, 0)
    return pl.pallas_call(
        functools.partial(_mla_down_kernel, q_lora=q_lora, kv_lora=kv_lora),
        grid=(n // tm,),
        in_specs=[
            pl.BlockSpec((tm, d), row),
            pl.BlockSpec((1, d), fix),
            pl.BlockSpec((d, wn), fix),
            pl.BlockSpec((1, q_lora), fix),
            pl.BlockSpec((1, kv_lora), fix),
            pl.BlockSpec((tm, LANES), row),
            pl.BlockSpec((tm, LANES), row),
        ],
        out_specs=[
            pl.BlockSpec((tm, q_lora), row),
            pl.BlockSpec((tm, kv_lora), row),
            pl.BlockSpec((tm, kv_lora), row),
            pl.BlockSpec((tm, ROPE_DIM), row),
            pl.BlockSpec((tm, 2 * ROPE_DIM), row),
        ],
        out_shape=[
            jax.ShapeDtypeStruct((n, q_lora), BF16),
            jax.ShapeDtypeStruct((n, kv_lora), F32),
            jax.ShapeDtypeStruct((n, kv_lora), BF16),
            jax.ShapeDtypeStruct((n, ROPE_DIM), F32),
            jax.ShapeDtypeStruct((n, 2 * ROPE_DIM), BF16),
        ],
        compiler_params=_params("parallel"),
        name="mla_down",
    )(x, g.reshape(1, d), w_ext, gq.reshape(1, -1), gkv.reshape(1, -1), cosz, sinz)


def _mla_q_kernel(a_ref, w_ref, cosz_ref, sinz_ref, o_ref):
    y = _dot(a_ref[...], w_ref[...])
    cosz = cosz_ref[...]
    sinz = sinz_ref[...]
    for c in range(0, y.shape[1], 2 * HEAD_DIM):
        rot = _rope_group(y[:, c + HEAD_DIM:c + 2 * HEAD_DIM], cosz, sinz)
        o_ref[:, c:c + HEAD_DIM] = y[:, c:c + HEAD_DIM].astype(BF16)
        o_ref[:, c + HEAD_DIM:c + 2 * HEAD_DIM] = rot.astype(BF16)


def _mla_q(qn, w_ext, cosz, sinz):
    n, k = qn.shape
    nout = w_ext.shape[1]
    tm = _pick(n, (512, 256, 128))
    tn = _pick(nout, (2048, 1024, 512, 256))
    return pl.pallas_call(
        _mla_q_kernel,
        grid=(n // tm, nout // tn),
        in_specs=[
            pl.BlockSpec((tm, k), lambda i, j: (i, 0)),
            pl.BlockSpec((k, tn), lambda i, j: (0, j)),
            pl.BlockSpec((tm, LANES), lambda i, j: (i, 0)),
            pl.BlockSpec((tm, LANES), lambda i, j: (i, 0)),
        ],
        out_specs=pl.BlockSpec((tm, tn), lambda i, j: (i, j)),
        out_shape=jax.ShapeDtypeStruct((n, nout), BF16),
        compiler_params=_params("parallel", "parallel"),
        name="mla_q",
    )(qn, w_ext, cosz, sinz)


def _mla_kv_kernel(c_ref, wk_ref, wv_ref, kr_ref, k_ref, v_ref):
    c = c_ref[...].astype(BF16)
    yk = _dot(c, wk_ref[...]).astype(BF16)
    v_ref[...] = _dot(c, wv_ref[...]).astype(BF16)
    kr = kr_ref[...].astype(BF16)
    for hh in range(yk.shape[1] // HEAD_DIM):
        k_ref[:, 2 * hh * HEAD_DIM:(2 * hh + 1) * HEAD_DIM] = yk[:, hh * HEAD_DIM:(hh + 1) * HEAD_DIM]
        k_ref[:, (2 * hh + 1) * HEAD_DIM:(2 * hh + 2) * HEAD_DIM] = kr


def _mla_kv(c, w_uk, w_uv, kr):
    n, k = c.shape
    hd = w_uk.shape[1]
    tm = _pick(n, (512, 256, 128))
    tn = _pick(hd, (1024, 512, 256, 128))
    return pl.pallas_call(
        _mla_kv_kernel,
        grid=(n // tm, hd // tn),
        in_specs=[
            pl.BlockSpec((tm, k), lambda i, j: (i, 0)),
            pl.BlockSpec((k, tn), lambda i, j: (0, j)),
            pl.BlockSpec((k, tn), lambda i, j: (0, j)),
            pl.BlockSpec((tm, LANES), lambda i, j: (i, 0)),
        ],
        out_specs=[
            pl.BlockSpec((tm, 2 * tn), lambda i, j: (i, j)),
            pl.BlockSpec((tm, tn), lambda i, j: (i, j)),
        ],
        out_shape=[
            jax.ShapeDtypeStruct((n, 2 * hd), BF16),
            jax.ShapeDtypeStruct((n, hd), BF16),
        ],
        compiler_params=_params("parallel", "parallel"),
        name="mla_kv",
    )(c, w_uk, w_uv, kr)


def _softmax_pv(parts):
    m = parts[0][0].max(axis=-1, keepdims=True)
    for t, _ in parts[1:]:
        m = jnp.maximum(m, t.max(axis=-1, keepdims=True))
    l = None
    acc = None
    for t, v in parts:
        p = jnp.exp2(t - m)
        ps = p.sum(axis=-1, keepdims=True)
        pv = _dot(p.astype(BF16), v)
        l = ps if l is None else l + ps
        acc = pv if acc is None else acc + pv
    return acc / l


def _mla_prompt_kernel(q_ref, k_ref, v_ref, o_ref, *, tq, scale2):
    t = q_ref.shape[0]
    r = lax.broadcasted_iota(jnp.int32, (tq, tq), 0)
    c = lax.broadcasted_iota(jnp.int32, (tq, tq), 1)
    diag = c < (r | (CHUNK - 1)) + 1

    def scores(i, _):
        return _dot_nt(q_ref[i * tq:(i + 1) * tq, :], k_ref[0:(i + 1) * tq, :]) * scale2

    def finish(i, s):
        lo = i * tq
        parts = [(s[:, :lo], v_ref[0:lo, :])] if i else []
        parts.append((jnp.where(diag, s[:, lo:], NEG_INF), v_ref[lo:lo + tq, :]))
        o_ref[lo:lo + tq, :] = _softmax_pv(parts).astype(o_ref.dtype)

    _skewed(list(range(t // tq)), [scores, finish])


def _mla_prompt_attn(q, k, v, batch, t, heads, scale):
    tq = _pick(t, (256, 128))
    return pl.pallas_call(
        functools.partial(_mla_prompt_kernel, tq=tq, scale2=scale * LOG2E),
        grid=(batch, heads),
        in_specs=[
            pl.BlockSpec((t, 2 * HEAD_DIM), lambda b, h: (b, h)),
            pl.BlockSpec((t, 2 * HEAD_DIM), lambda b, h: (b, h)),
            pl.BlockSpec((t, HEAD_DIM), lambda b, h: (b, h)),
        ],
        out_specs=pl.BlockSpec((t, HEAD_DIM), lambda b, h: (b, h)),
        out_shape=jax.ShapeDtypeStruct((q.shape[0], heads * HEAD_DIM), BF16),
        compiler_params=_params("parallel", "parallel"),
        name="mla_attn_p",
    )(q, k, v)


def _mla_sample_kernel(q_ref, kp_ref, vp_ref, kn_ref, vn_ref, o_in_ref, o_ref, *, scale2):
    del o_in_ref
    q = q_ref[...]
    parts = [(_dot_nt(q, kp_ref[...]) * scale2, vp_ref[...]),
             (_dot_nt(q, kn_ref[...]) * scale2, vn_ref[...])]
    o_ref[...] = _softmax_pv(parts).astype(o_ref.dtype)


def _mla_sample_attn(q, k_past, v_past, k_all, v_all, o_prev, batch, t, past, heads, row0, scale):
    assert row0 % t == 0
    off = row0 // t
    new = lambda width: pl.BlockSpec((t, width), lambda b, h: (b + off, h))
    old = lambda width: pl.BlockSpec((past, width), lambda b, h: (b, h))
    return pl.pallas_call(
        functools.partial(_mla_sample_kernel, scale2=scale * LOG2E),
        grid=(batch, heads),
        in_specs=[new(2 * HEAD_DIM), old(2 * HEAD_DIM), old(HEAD_DIM), new(2 * HEAD_DIM), new(HEAD_DIM),
                  pl.BlockSpec(memory_space=pl.ANY)],
        out_specs=new(HEAD_DIM),
        out_shape=jax.ShapeDtypeStruct(o_prev.shape, o_prev.dtype),
        input_output_aliases={5: 0},
        compiler_params=_params("parallel", "parallel"),
        name="mla_attn_s",
    )(q, k_past, v_past, k_all, v_all, o_prev)


def _band_prompt_kernel(q_ref, k_ref, v_ref, bias_ref, o_ref, *, tq, scale2):
    t = q_ref.shape[0]
    band = bias_ref.shape[2]

    def scores(i, _):
        ke = (i + 1) * tq
        ks = max(0, ke - band)
        s = _dot_nt(q_ref[i * tq:(i + 1) * tq, :], k_ref[ks:ke, :])
        return s * scale2 + bias_ref[0, :, band - (ke - ks):]

    def finish(i, s):
        ke = (i + 1) * tq
        ks = max(0, ke - band)
        o_ref[i * tq:(i + 1) * tq, :] = _softmax_pv([(s, v_ref[ks:ke, :])]).astype(o_ref.dtype)

    _skewed(list(range(t // tq)), [scores, finish])


def _band_prompt_attn(qkv, bias2, batch, t, heads, scale):
    return pl.pallas_call(
        functools.partial(_band_prompt_kernel, tq=bias2.shape[1], scale2=scale * LOG2E),
        grid=(batch, heads),
        in_specs=[
            pl.BlockSpec((t, HEAD_DIM), lambda b, h: (b, h)),
            pl.BlockSpec((t, HEAD_DIM), lambda b, h: (b, h + heads)),
            pl.BlockSpec((t, HEAD_DIM), lambda b, h: (b, h + 2 * heads)),
            pl.BlockSpec((1,) + bias2.shape[1:], lambda b, h: (h, 0, 0)),
        ],
        out_specs=pl.BlockSpec((t, HEAD_DIM), lambda b, h: (b, h)),
        out_shape=jax.ShapeDtypeStruct((qkv.shape[0], heads * HEAD_DIM), BF16),
        compiler_params=_params("parallel", "parallel"),
        name="band_attn_p",
    )(qkv, qkv, qkv, bias2)


def _band_sample_kernel(q_ref, kp_ref, vp_ref, kn_ref, vn_ref, bias_ref, o_in_ref, o_ref, *, scale2):
    del o_in_ref
    win = kp_ref.shape[0]
    q = q_ref[...]
    parts = [(_dot_nt(q, kp_ref[...]) * scale2 + bias_ref[0, :, :win], vp_ref[...]),
             (_dot_nt(q, kn_ref[...]) * scale2 + bias_ref[0, :, win:], vn_ref[...])]
    o_ref[...] = _softmax_pv(parts).astype(o_ref.dtype)


def _cache_rows(cache):
    b, rows = cache.shape[:2]
    return cache.reshape(b * rows, -1).astype(BF16)


def _cache_spec(rows, col0):
    return pl.BlockSpec((rows, HEAD_DIM), lambda b, h: (b, h + col0))


def _band_sample_attn(qkv, cache, bias, o_prev, batch, t, heads, row0, scale):
    assert row0 % t == 0
    off = row0 // t
    win = cache.shape[0] // batch
    new = lambda col0: pl.BlockSpec((t, HEAD_DIM), lambda b, h: (b + off, h + col0))
    return pl.pallas_call(
        functools.partial(_band_sample_kernel, scale2=scale * LOG2E),
        grid=(batch, heads),
        in_specs=[new(0), _cache_spec(win, 0), _cache_spec(win, heads), new(heads), new(2 * heads),
                  pl.BlockSpec((1,) + bias.shape[1:], lambda b, h: (h, 0, 0)),
                  pl.BlockSpec(memory_space=pl.ANY)],
        out_specs=pl.BlockSpec((t, HEAD_DIM), lambda b, h: (b + off, h)),
        out_shape=jax.ShapeDtypeStruct(o_prev.shape, o_prev.dtype),
        input_output_aliases={6: 0},
        compiler_params=_params("parallel", "parallel"),
        name="band_attn_s",
    )(qkv, cache, cache, qkv, qkv, bias, o_prev)


def _sb_items(n_q, tq, tk, lead):
    items = []
    for i in range(n_q):
        blocks = [(0, i * tq, tq, True)]
        end = i * tq
        while end > 0:
            w = min(tk, end)
            blocks.append((0, end - w, w, False))
            end -= w
        end = lead
        while end > 0:
            w = min(tk, end)
            blocks.append((1, end - w, w, False))
            end -= w
        for n, (src, k0, w, own) in enumerate(blocks):
            items.append((i, src, k0, w, own, n == 0, n == len(blocks) - 1))
    return items


def _sb_body(q_ref, srcs, o_ref, *, tq, tk, lead, scale2):
    t = q_ref.shape[0]
    tri_n = max(tq, tk)
    row = lax.broadcasted_iota(jnp.int32, (tri_n, tri_n), 0)
    col = lax.broadcasted_iota(jnp.int32, (tri_n, tri_n), 1)
    tri = jnp.where(row > col, 1.0, 0.0).astype(BF16)
    keep = (lax.broadcasted_iota(jnp.int32, (tq, tq), 1)
            < lax.broadcasted_iota(jnp.int32, (tq, tq), 0))
    state = {}

    def terms(item, _):
        i, src, k0, w, own, _, _ = item
        k = srcs[src][0][k0:k0 + w, :]
        nz = _dot_nt(q_ref[i * tq:(i + 1) * tq, :], k) * (-scale2)
        neg_abs = lax.bitcast_convert_type(
            lax.bitcast_convert_type(nz, jnp.uint32) | jnp.uint32(0x80000000), F32)
        lf = jnp.minimum(nz, 0.0) - jnp.log(1.0 + jnp.exp2(neg_abs)) * LOG2E
        if own:
            lf = jnp.where(keep, lf, 0.0)
        hi = lf.astype(BF16)
        lo = (lf - hi.astype(F32)).astype(BF16)
        return nz, lf, (hi, lo), lf.sum(axis=-1, keepdims=True)

    def suffix(item, st):
        w = item[3]
        hi, lo = st[2]
        tri_w = tri[:w, :w]
        if w % LANES == 0:
            within = _dot(jnp.concatenate([hi, lo], axis=1), jnp.concatenate([tri_w, tri_w], axis=0))
        else:
            within = _dot(hi, tri_w) + _dot(lo, tri_w)
        return st + (within,)

    def weigh(item, st):
        i, src, k0, w, own, first, last = item
        nz, lf, _, total, within = st
        if first:
            state["carry"] = jnp.zeros((tq, 1), F32)
            state["acc"] = jnp.zeros((tq, HEAD_DIM), F32)
        a = jnp.exp2((lf - nz) + (within + state["carry"]))
        if own:
            a = jnp.where(keep, a, 0.0)
        state["acc"] = state["acc"] + _dot(a.astype(BF16), srcs[src][1][k0:k0 + w, :])
        state["carry"] = state["carry"] + total
        if last:
            o_ref[i * tq:(i + 1) * tq, :] = state["acc"].astype(o_ref.dtype)

    _skewed(_sb_items(t // tq, tq, tk, lead), [terms, suffix, weigh])


def _sb_prompt_kernel(q_ref, k_ref, v_ref, o_ref, **kw):
    _sb_body(q_ref, [(k_ref, v_ref)], o_ref, lead=0, **kw)


def _sb_sample_kernel(q_ref, kn_ref, vn_ref, kp_ref, vp_ref, o_in_ref, o_ref, **kw):
    del o_in_ref
    _sb_body(q_ref, [(kn_ref, vn_ref), (kp_ref, vp_ref)], o_ref, lead=kp_ref.shape[0], **kw)


def _sb_prompt_attn(qkv, batch, t, heads, scale):
    tq = _pick(t, (256, 128, 64))
    return pl.pallas_call(
        functools.partial(_sb_prompt_kernel, tq=tq, tk=MXU_DIM, scale2=scale * LOG2E),
        grid=(batch, heads),
        in_specs=[
            pl.BlockSpec((t, HEAD_DIM), lambda b, h: (b, h)),
            pl.BlockSpec((t, HEAD_DIM), lambda b, h: (b, h + heads)),
            pl.BlockSpec((t, HEAD_DIM), lambda b, h: (b, h + 2 * heads)),
        ],
        out_specs=pl.BlockSpec((t, HEAD_DIM), lambda b, h: (b, h)),
        out_shape=jax.ShapeDtypeStruct((qkv.shape[0], heads * HEAD_DIM), BF16),
        compiler_params=_params("parallel", "parallel"),
        name="sb_attn_p",
    )(qkv, qkv, qkv)


def _sb_sample_attn(qkv, cache, o_prev, batch, t, heads, row0, scale):
    assert row0 % t == 0
    off = row0 // t
    past = cache.shape[0] // batch
    new = lambda col0: pl.BlockSpec((t, HEAD_DIM), lambda b, h: (b + off, h + col0))
    return pl.pallas_call(
        functools.partial(_sb_sample_kernel, tq=t, tk=MXU_DIM, scale2=scale * LOG2E),
        grid=(batch, heads),
        in_specs=[new(0), new(heads), new(2 * heads), _cache_spec(past, 0), _cache_spec(past, heads),
                  pl.BlockSpec(memory_space=pl.ANY)],
        out_specs=pl.BlockSpec((t, HEAD_DIM), lambda b, h: (b + off, h)),
        out_shape=jax.ShapeDtypeStruct(o_prev.shape, o_prev.dtype),
        input_output_aliases={5: 0},
        compiler_params=_params("parallel", "parallel"),
        name="sb_attn_s",
    )(qkv, qkv, qkv, cache, cache, o_prev)


def _rope_tables(pos):
    half = ROPE_DIM // 2
    inv_freq = ROPE_THETA ** (-jnp.arange(half, dtype=F32) / half)
    ang = pos.astype(F32)[:, None] * inv_freq[None, :]
    cos = jnp.cos(ang)
    sin = jnp.sin(ang)
    zero = jnp.zeros((pos.shape[0], LANES - ROPE_DIM), F32)
    return jnp.concatenate([cos, cos, zero], axis=1), jnp.concatenate([sin, sin, zero], axis=1)


def _rotate_cols(w):
    half = ROPE_DIM // 2
    return jnp.concatenate([-w[..., half:], w[..., :half]], axis=-1)


def _mla_layer(x, g, cosz, sinz, past_ckv, past_krope, w_down, g_q, w_uq, g_kv, w_uk, w_uv, w_o, dims):
    bp, tp, bs, ts = dims
    np_rows = bp * tp
    q_lora = g_q.shape[0]
    kv_lora = g_kv.shape[0]
    heads = w_uk.shape[1]
    past = past_ckv.shape[1]
    scale = (HEAD_DIM + ROPE_DIM) ** -0.5

    w_rope = w_down[:, q_lora + kv_lora:]
    w_down_ext = jnp.concatenate([w_down, _rotate_cols(w_rope)], axis=1).astype(BF16)
    wq = w_uq.reshape(q_lora, heads, HEAD_DIM + ROPE_DIM)
    w_uq_ext = jnp.concatenate([wq, _rotate_cols(wq[..., HEAD_DIM:])], axis=-1)
    w_uq_ext = w_uq_ext.reshape(q_lora, heads * 2 * HEAD_DIM).astype(BF16)
    w_uk2 = w_uk.reshape(kv_lora, heads * HEAD_DIM).astype(BF16)
    w_uv2 = w_uv.reshape(kv_lora, heads * HEAD_DIM).astype(BF16)

    qn, ckv, ckv_b, kr, kr_b = _mla_down(x, g, w_down_ext, g_q, g_kv, cosz, sinz, q_lora, kv_lora)
    q = _mla_q(qn, w_uq_ext, cosz, sinz)
    k_all, v_all = _mla_kv(ckv_b, w_uk2, w_uv2, kr_b)
    kr_past = jnp.pad(past_krope.reshape(bs * past, ROPE_DIM), ((0, 0), (0, LANES - ROPE_DIM)))
    k_past, v_past = _mla_kv(past_ckv.reshape(bs * past, kv_lora), w_uk2, w_uv2, kr_past)

    o = _mla_prompt_attn(q, k_all, v_all, bp, tp, heads, scale)
    o = _mla_sample_attn(q, k_past, v_past, k_all, v_all, o, bs, ts, past, heads, np_rows, scale)
    x = _mm_res(o, w_o.astype(BF16), x)
    outs = (ckv[:np_rows].reshape(bp, tp, kv_lora), kr[:np_rows].reshape(bp, tp, ROPE_DIM),
            ckv[np_rows:].reshape(bs, ts, kv_lora), kr[np_rows:].reshape(bs, ts, ROPE_DIM))
    return x, outs


def _band_bias(rel_bias, nch):
    band = LEFT_CTX + CHUNK
    far = band - 1 - REL_CLIP
    u = jnp.concatenate([jnp.repeat(rel_bias[:, 2 * REL_CLIP:], far, axis=1),
                         rel_bias[:, REL_CLIP - (CHUNK - 1):2 * REL_CLIP + 1][:, ::-1]], axis=1) * LOG2E
    bias = jnp.stack([u[:, CHUNK - 1 - q:CHUNK - 1 - q + band] for q in range(CHUNK)], axis=1).astype(F32)
    rows = [jnp.pad(bias, ((0, 0), (0, 0), (c * CHUNK, (nch - 1 - c) * CHUNK)), constant_values=NEG_INF)
            for c in range(nch)]
    return jnp.concatenate(rows, axis=1)


def _band_layer(x, g, cache, w_qkv, rel_bias, w_o, dims):
    bp, tp, bs, ts = dims
    np_rows = bp * tp
    heads = rel_bias.shape[0]
    hd = heads * HEAD_DIM
    win = cache.shape[1]
    scale = HEAD_DIM ** -0.5

    qkv_b, kv = _qkv(x, g, w_qkv.astype(BF16), hd)
    o = _band_prompt_attn(qkv_b, _band_bias(rel_bias, 2), bp, tp, heads, scale)
    o = _band_sample_attn(qkv_b, _cache_rows(cache), _band_bias(rel_bias, 1), o, bs, ts, heads, np_rows, scale)
    x = _mm_res(o, w_o.astype(BF16), x)

    wp = min(LEFT_CTX, tp)
    st_p = kv[:np_rows].reshape(bp, tp, 2, heads, HEAD_DIM)[:, tp - wp:]
    new = kv[np_rows:].reshape(bs, ts, 2, heads, HEAD_DIM)
    st_s = jnp.concatenate([cache, new], axis=1)[:, ts:]
    return x, (st_p, st_s)


def _sb_layer(x, g, cache, w_qkv, w_o, dims):
    bp, tp, bs, ts = dims
    np_rows = bp * tp
    hd = w_o.shape[0]
    heads = hd // HEAD_DIM
    scale = HEAD_DIM ** -0.5

    qkv_b, kv = _qkv(x, g, w_qkv.astype(BF16), hd)
    o = _sb_prompt_attn(qkv_b, bp, tp, heads, scale)
    o = _sb_sample_attn(qkv_b, _cache_rows(cache), o, bs, ts, heads, np_rows, scale)
    x = _mm_res(o, w_o.astype(BF16), x)
    st_p = kv[:np_rows].reshape(bp, tp, 2, heads, HEAD_DIM)
    st_s = kv[np_rows:].reshape(bs, ts, 2, heads, HEAD_DIM)
    return x, (st_p, st_s)


def kernel(x_prompt, x_sample, cache_mla_ckv, cache_mla_krope, cache_band_kv, cache_sb_kv, norm_g, final_norm_g, ffn_w_gu, ffn_w_down, mla_w_down, mla_q_norm_g, mla_kv_norm_g, mla_w_uq, mla_w_uk, mla_w_uv, mla_w_o, band_w_qkv, band_rel_bias, band_w_o, sb_w_qkv, sb_w_o):
    bp, tp, d = x_prompt.shape
    bs, ts, _ = x_sample.shape
    depth = norm_g.shape[0]
    past = cache_mla_ckv.shape[2]
    dims = (bp, tp, bs, ts)
    np_rows = bp * tp
    ns_rows = bs * ts

    x = jnp.concatenate([x_prompt.reshape(np_rows, d), x_sample.reshape(ns_rows, d)], axis=0)
    w_gu = ffn_w_gu[0, 0].astype(BF16)
    w_dn = ffn_w_down[0, 0].astype(BF16)

    def ffn(x, g, w_gu, w_dn, k):
        if k + 1 == 2 * depth:
            return _ffn(x, g, w_gu, w_dn), None, None
        return _ffn(x, g, w_gu, w_dn, (ffn_w_gu, ffn_w_down, (k + 1) // 2, (k + 1) % 2))

    pos = jnp.concatenate([jnp.tile(jnp.arange(tp), bp), jnp.tile(past + jnp.arange(ts), bs)])
    cosz, sinz = _rope_tables(pos)

    ckv_p, krope_p, band_p, sb_p = [], [], [], []
    ckv_s, krope_s, band_s, sb_s = [], [], [], []
    for i in range(depth):
        kind, j = i % 3, i // 3
        x, w_gu, w_dn = ffn(x, norm_g[i, 0], w_gu, w_dn, 2 * i)
        if kind == 0:
            x, (c_p, k_p, c_s, k_s) = _mla_layer(
                x, norm_g[i, 1], cosz, sinz, cache_mla_ckv[j], cache_mla_krope[j],
                mla_w_down[j], mla_q_norm_g[j], mla_w_uq[j], mla_kv_norm_g[j], mla_w_uk[j], mla_w_uv[j],
                mla_w_o[j], dims)
            ckv_p.append(c_p); krope_p.append(k_p); ckv_s.append(c_s); krope_s.append(k_s)
        elif kind == 1:
            x, (st_p, st_s) = _band_layer(x, norm_g[i, 1], cache_band_kv[j], band_w_qkv[j], band_rel_bias[j],
                                          band_w_o[j], dims)
            band_p.append(st_p); band_s.append(st_s)
        else:
            x, (st_p, st_s) = _sb_layer(x, norm_g[i, 1], cache_sb_kv[j], sb_w_qkv[j], sb_w_o[j], dims)
            sb_p.append(st_p); sb_s.append(st_s)
        x, w_gu, w_dn = ffn(x, norm_g[i, 2], w_gu, w_dn, 2 * i + 1)

    y_prompt = _final_norm(x, final_norm_g, 0, np_rows).reshape(bp, tp, d)
    y_sample = _final_norm(x, final_norm_g, np_rows, ns_rows).reshape(bs, ts, d)
    return (y_prompt, y_sample,
            jnp.stack(ckv_p), jnp.stack(krope_p), jnp.stack(band_p), jnp.stack(sb_p),
            jnp.stack(ckv_s), jnp.stack(krope_s), jnp.stack(band_s), jnp.stack(sb_s))
```

```python
import functools
import math

import jax
import jax.numpy as jnp
from jax import lax
from jax.experimental import pallas as pl
from jax.experimental.pallas import tpu as pltpu

F32 = jnp.float32
BF16 = jnp.bfloat16

EPS = 1e-6
NEG_INF = -1e30
CHUNK = 64
HEAD_DIM = 128
ROPE_DIM = 64
ROPE_THETA = 10000.0
LEFT_CHUNKS = 8
LEFT_CTX = LEFT_CHUNKS * CHUNK
REL_CLIP = 128
LANES = 128
MXU_DIM = 256
HEAD_GROUP = 8
LOG2E = math.log2(math.e)
VMEM_LIMIT = 56 * 1024 * 1024
FFN_VMEM_LIMIT = 60 * 1024 * 1024


def _pick(n, candidates):
    for c in candidates:
        if n % c == 0:
            return c
    raise ValueError(f"no tile in {candidates} divides {n}")


def _params(*sem, vmem=VMEM_LIMIT):
    return pltpu.CompilerParams(dimension_semantics=sem, vmem_limit_bytes=vmem)


def _rms(x, g):
    return x * lax.rsqrt(jnp.mean(x * x, axis=-1, keepdims=True) + EPS) * g


def _dot(a, b):
    return jnp.dot(a, b, preferred_element_type=F32)


def _dot_nt(a, b):
    return lax.dot_general(a, b, (((1,), (1,)), ((), ())), preferred_element_type=F32)


def _skewed(items, stages):
    n, ns = len(items), len(stages)
    live = {}
    for step in range(n + ns - 1):
        for s in range(ns):
            t = step - s
            if 0 <= t < n:
                live[t] = stages[s](items[t], live.get(t))
        live.pop(step - ns + 1, None)


def _ffn_kernel(*refs, n_col, cast_next):
    x_ref, g_ref, wg_ref, wu_ref, wd_ref = refs[:5]
    if cast_next:
        ngu_ref, ndn_ref, o_ref, ngu_out, ndn_out, h_ref = refs[5:]
    else:
        o_ref, h_ref = refs[5:]

    @pl.when(pl.program_id(1) == 0)
    def _():
        x = x_ref[...]
        h_ref[...] = _rms(x, g_ref[...]).astype(BF16)
        o_ref[...] = x

    h = h_ref[...]
    gate = _dot(h, wg_ref[...])
    if cast_next:
        ngu_out[...] = ngu_ref[...].astype(BF16)
        ndn_out[...] = ndn_ref[...].astype(BF16)
    up = _dot(h, wu_ref[...])
    a = (gate * jax.nn.sigmoid(gate) * up).astype(BF16)
    d = o_ref.shape[1]
    for n in range(0, d, n_col):
        o_ref[:, n:n + n_col] += 0.5 * _dot(a, wd_ref[:, n:n + n_col])


def _ffn(x, g, w_gu, w_down, nxt=None):
    n, d = x.shape
    d_ff = w_down.shape[0]
    tm = _pick(n, (512, 256, 128))
    tf = _pick(d_ff, (256, 128))
    ni, nf = n // tm, d_ff // tf
    in_specs = [
        pl.BlockSpec((tm, d), lambda i, j: (i, 0)),
        pl.BlockSpec((1, d), lambda i, j: (0, 0)),
        pl.BlockSpec((d, tf), lambda i, j: (0, j)),
        pl.BlockSpec((d, tf), lambda i, j: (0, j + nf)),
        pl.BlockSpec((tf, d), lambda i, j: (j, 0)),
    ]
    out_specs = [pl.BlockSpec((tm, d), lambda i, j: (i, 0))]
    out_shape = [jax.ShapeDtypeStruct((n, d), F32)]
    args = [x, g.reshape(1, d), w_gu, w_gu, w_down]
    if nxt is not None:
        next_gu, next_dn, layer, half = nxt
        cr = min(1 << (ni.bit_length() - 1), d // LANES)
        rows, cols = d // cr, 2 * tf
        slab_i = lambda i: jnp.minimum(i, cr - 1)
        slab_j = lambda i, j: jnp.where(i < cr, j, nf - 1)
        in_specs += [
            pl.BlockSpec((None, None, rows, cols), lambda i, j: (layer, half, slab_i(i), slab_j(i, j))),
            pl.BlockSpec((None, None, tf, rows), lambda i, j: (layer, half, slab_j(i, j), slab_i(i))),
        ]
        out_specs += [
            pl.BlockSpec((rows, cols), lambda i, j: (slab_i(i), slab_j(i, j))),
            pl.BlockSpec((tf, rows), lambda i, j: (slab_j(i, j), slab_i(i))),
        ]
        out_shape += [jax.ShapeDtypeStruct((d, 2 * d_ff), BF16), jax.ShapeDtypeStruct((d_ff, d), BF16)]
        args += [next_gu, next_dn]
    out = pl.pallas_call(
        functools.partial(_ffn_kernel, n_col=_pick(d, (512, 256, 128)), cast_next=nxt is not None),
        grid=(ni, nf),
        in_specs=in_specs,
        out_specs=out_specs,
        out_shape=out_shape,
        scratch_shapes=[pltpu.VMEM((tm, d), BF16)],
        compiler_params=_params("arbitrary", "arbitrary", vmem=FFN_VMEM_LIMIT),
        name="ffn",
    )(*args)
    return out if nxt is not None else out[0]


def _qkv_kernel(*refs, aliased):
    x_ref, g_ref, w_ref = refs[:3]
    ob_ref, of_ref, h_ref = refs[4:] if aliased else refs[3:]
    j = pl.program_id(1)

    @pl.when(j == 0)
    def _():
        h_ref[...] = _rms(x_ref[...], g_ref[...]).astype(BF16)

    y = _dot(h_ref[...], w_ref[...])
    ob_ref[...] = y.astype(BF16)
    of_ref[...] = y.reshape(of_ref.shape)


def _qkv(x, g, w, q_cols, row0, rows, qkv_prev=None):
    n, d = x.shape
    nout = w.shape[1]
    tm = _pick(rows, (512, 256, 128))
    tn = _pick(math.gcd(q_cols, nout), (1024, 512, 256, 128))
    qb = q_cols // tn
    assert row0 % tm == 0
    off = row0 // tm
    in_specs = [
        pl.BlockSpec((tm, d), lambda i, j: (i + off, 0)),
        pl.BlockSpec((1, d), lambda i, j: (0, 0)),
        pl.BlockSpec((d, tn), lambda i, j: (0, j)),
    ]
    args = [x, g.reshape(1, d), w]
    if qkv_prev is not None:
        in_specs.append(pl.BlockSpec(memory_space=pl.ANY))
        args.append(qkv_prev)
    return pl.pallas_call(
        functools.partial(_qkv_kernel, aliased=qkv_prev is not None),
        grid=(rows // tm, nout // tn),
        in_specs=in_specs,
        out_specs=[
            pl.BlockSpec((tm, tn), lambda i, j: (i + off, j)),
            pl.BlockSpec((tm, tn // HEAD_DIM, HEAD_DIM), lambda i, j: (i, jnp.maximum(j - qb, 0), 0)),
        ],
        out_shape=[
            jax.ShapeDtypeStruct((n, nout), BF16),
            jax.ShapeDtypeStruct((rows, (nout - q_cols) // HEAD_DIM, HEAD_DIM), F32),
        ],
        input_output_aliases={3: 0} if qkv_prev is not None else {},
        scratch_shapes=[pltpu.VMEM((tm, d), BF16)],
        compiler_params=_params("parallel", "arbitrary"),
        name="qkv",
    )(*args)


def _mm_res_kernel(a_ref, w_ref, r_ref, o_ref):
    o_ref[...] = r_ref[...] + _dot(a_ref[...], w_ref[...])


def _mm_res(a, w, res):
    n, k = a.shape
    nout = w.shape[1]
    tm = _pick(n, (512, 256, 128))
    tn = _pick(nout, (1024, 512, 256, 128))
    return pl.pallas_call(
        _mm_res_kernel,
        grid=(n // tm, nout // tn),
        in_specs=[
            pl.BlockSpec((tm, k), lambda i, j: (i, 0)),
            pl.BlockSpec((k, tn), lambda i, j: (0, j)),
            pl.BlockSpec((tm, tn), lambda i, j: (i, j)),
        ],
        out_specs=pl.BlockSpec((tm, tn), lambda i, j: (i, j)),
        out_shape=jax.ShapeDtypeStruct((n, nout), F32),
        compiler_params=_params("parallel", "parallel"),
        name="mm_res",
    )(a, w, res)


def _final_norm_kernel(x_ref, g_ref, o_ref):
    o_ref[...] = _rms(x_ref[...], g_ref[...])


def _final_norm(x, g, row0, rows):
    d = x.shape[1]
    tm = _pick(rows, (512, 256, 128))
    assert row0 % tm == 0
    off = row0 // tm
    return pl.pallas_call(
        _final_norm_kernel,
        grid=(rows // tm,),
        in_specs=[
            pl.BlockSpec((tm, d), lambda i: (i + off, 0)),
            pl.BlockSpec((1, d), lambda i: (0, 0)),
        ],
        out_specs=pl.BlockSpec((tm, d), lambda i: (i, 0)),
        out_shape=jax.ShapeDtypeStruct((rows, d), F32),
        compiler_params=_params("parallel"),
        name="final_norm",
    )(x, g.reshape(1, d))


def _rope_group(r, cosz, sinz):
    return r * cosz + pltpu.roll(r, ROPE_DIM, axis=1) * sinz


def _mla_down_kernel(x_ref, g_ref, w_ref, gq_ref, gkv_ref, cosz_ref, sinz_ref,
                     qn_ref, ckv_ref, ckvb_ref, kr_ref, krb_ref, *, q_lora, kv_lora):
    h = _rms(x_ref[...], g_ref[...]).astype(BF16)
    y = _dot(h, w_ref[...])
    qn_ref[...] = _rms(y[:, :q_lora], gq_ref[...]).astype(BF16)
    ckv = _rms(y[:, q_lora:q_lora + kv_lora], gkv_ref[...])
    ckv_ref[...] = ckv
    ckvb_ref[...] = ckv.astype(BF16)
    r0 = q_lora + kv_lora
    kr = _rope_group(y[:, r0:r0 + 2 * ROPE_DIM], cosz_ref[...], sinz_ref[...])
    kr_ref[...] = kr[:, :ROPE_DIM]
    krb_ref[...] = kr.astype(BF16)


def _mla_down(x, g, w_ext, gq, gkv, cosz, sinz, q_lora, kv_lora):
    n, d = x.shape
    wn = w_ext.shape[1]
    tm = _pick(n, (256, 128))
    row = lambda i: (i, 0)
    fix = lambda i: (0, 0)
    return pl.pallas_call(
        functools.partial(_mla_down_kernel, q_lora=q_lora, kv_lora=kv_lora),
        grid=(n // tm,),
        in_specs=[
            pl.BlockSpec((tm, d), row),
            pl.BlockSpec((1, d), fix),
            pl.BlockSpec((d, wn), fix),
            pl.BlockSpec((1, q_lora), fix),
            pl.BlockSpec((1, kv_lora), fix),
            pl.BlockSpec((tm, LANES), row),
            pl.BlockSpec((tm, LANES), row),
        ],
        out_specs=[
            pl.BlockSpec((tm, q_lora), row),
            pl.BlockSpec((tm, kv_lora), row),
            pl.BlockSpec((tm, kv_lora), row),
            pl.BlockSpec((tm, ROPE_DIM), row),
            pl.BlockSpec((tm, 2 * ROPE_DIM), row),
        ],
        out_shape=[
            jax.ShapeDtypeStruct((n, q_lora), BF16),
            jax.ShapeDtypeStruct((n, kv_lora), F32),
            jax.ShapeDtypeStruct((n, kv_lora), BF16),
            jax.ShapeDtypeStruct((n, ROPE_DIM), F32),
            jax.ShapeDtypeStruct((n, 2 * ROPE_DIM), BF16),
        ],
        compiler_params=_params("parallel"),
        name="mla_down",
    )(x, g.reshape(1, d), w_ext, gq.reshape(1, -1), gkv.reshape(1, -1), cosz, sinz)


def _mla_q_kernel(a_ref, w_ref, cosz_ref, sinz_ref, o_ref):
    y = _dot(a_ref[...], w_ref[...])
    cosz = cosz_ref[...]
    sinz = sinz_ref[...]
    for c in range(0, y.shape[1], 2 * HEAD_DIM):
        rot = _rope_group(y[:, c + HEAD_DIM:c + 2 * HEAD_DIM], cosz, sinz)
        o_ref[:, c:c + HEAD_DIM] = y[:, c:c + HEAD_DIM].astype(BF16)
        o_ref[:, c + HEAD_DIM:c + 2 * HEAD_DIM] = rot.astype(BF16)


def _mla_q(qn, w_ext, cosz, sinz):
    n, k = qn.shape
    nout = w_ext.shape[1]
    tm = _pick(n, (512, 256, 128))
    tn = _pick(nout, (2048, 1024, 512, 256))
    return pl.pallas_call(
        _mla_q_kernel,
        grid=(n // tm, nout // tn),
        in_specs=[
            pl.BlockSpec((tm, k), lambda i, j: (i, 0)),
            pl.BlockSpec((k, tn), lambda i, j: (0, j)),
            pl.BlockSpec((tm, LANES), lambda i, j: (i, 0)),
            pl.BlockSpec((tm, LANES), lambda i, j: (i, 0)),
        ],
        out_specs=pl.BlockSpec((tm, tn), lambda i, j: (i, j)),
        out_shape=jax.ShapeDtypeStruct((n, nout), BF16),
        compiler_params=_params("parallel", "parallel"),
        name="mla_q",
    )(qn, w_ext, cosz, sinz)


def _mla_kv_kernel(c_ref, wk_ref, wv_ref, kr_ref, k_ref, v_ref):
    c = c_ref[...].astype(BF16)
    yk = _dot(c, wk_ref[...]).astype(BF16)
    v_ref[...] = _dot(c, wv_ref[...]).astype(BF16)
    kr = kr_ref[...].astype(BF16)
    for hh in range(yk.shape[1] // HEAD_DIM):
        k_ref[:, 2 * hh * HEAD_DIM:(2 * hh + 1) * HEAD_DIM] = yk[:, hh * HEAD_DIM:(hh + 1) * HEAD_DIM]
        k_ref[:, (2 * hh + 1) * HEAD_DIM:(2 * hh + 2) * HEAD_DIM] = kr


def _mla_kv(c, w_uk, w_uv, kr):
    n, k = c.shape
    hd = w_uk.shape[1]
    tm = _pick(n, (512, 256, 128))
    tn = _pick(hd, (1024, 512, 256, 128))
    return pl.pallas_call(
        _mla_kv_kernel,
        grid=(n // tm, hd // tn),
        in_specs=[
            pl.BlockSpec((tm, k), lambda i, j: (i, 0)),
            pl.BlockSpec((k, tn), lambda i, j: (0, j)),
            pl.BlockSpec((k, tn), lambda i, j: (0, j)),
            pl.BlockSpec((tm, LANES), lambda i, j: (i, 0)),
        ],
        out_specs=[
            pl.BlockSpec((tm, 2 * tn), lambda i, j: (i, j)),
            pl.BlockSpec((tm, tn), lambda i, j: (i, j)),
        ],
        out_shape=[
            jax.ShapeDtypeStruct((n, 2 * hd), BF16),
            jax.ShapeDtypeStruct((n, hd), BF16),
        ],
        compiler_params=_params("parallel", "parallel"),
        name="mla_kv",
    )(c, w_uk, w_uv, kr)


def _softmax_pv(parts):
    m = parts[0][0].max(axis=-1, keepdims=True)
    for t, _ in parts[1:]:
        m = jnp.maximum(m, t.max(axis=-1, keepdims=True))
    l = None
    acc = None
    for t, v in parts:
        p = jnp.exp2(t - m)
        ps = p.sum(axis=-1, keepdims=True)
        pv = _dot(p.astype(BF16), v)
        l = ps if l is None else l + ps
        acc = pv if acc is None else acc + pv
    return acc / l


def _mla_prompt_kernel(q_ref, k_ref, v_ref, o_ref, *, tq, scale2):
    t = q_ref.shape[0]
    r = lax.broadcasted_iota(jnp.int32, (tq, tq), 0)
    c = lax.broadcasted_iota(jnp.int32, (tq, tq), 1)
    diag = c < (r | (CHUNK - 1)) + 1

    def scores(i, _):
        return _dot_nt(q_ref[i * tq:(i + 1) * tq, :], k_ref[0:(i + 1) * tq, :]) * scale2

    def finish(i, s):
        lo = i * tq
        parts = [(s[:, :lo], v_ref[0:lo, :])] if i else []
        parts.append((jnp.where(diag, s[:, lo:], NEG_INF), v_ref[lo:lo + tq, :]))
        o_ref[lo:lo + tq, :] = _softmax_pv(parts).astype(o_ref.dtype)

    _skewed(list(range(t // tq)), [scores, finish])


def _mla_prompt_attn(q, k, v, batch, t, heads, scale):
    tq = _pick(t, (256, 128))
    return pl.pallas_call(
        functools.partial(_mla_prompt_kernel, tq=tq, scale2=scale * LOG2E),
        grid=(batch, heads),
        in_specs=[
            pl.BlockSpec((t, 2 * HEAD_DIM), lambda b, h: (b, h)),
            pl.BlockSpec((t, 2 * HEAD_DIM), lambda b, h: (b, h)),
            pl.BlockSpec((t, HEAD_DIM), lambda b, h: (b, h)),
        ],
        out_specs=pl.BlockSpec((t, HEAD_DIM), lambda b, h: (b, h)),
        out_shape=jax.ShapeDtypeStruct((q.shape[0], heads * HEAD_DIM), BF16),
        compiler_params=_params("parallel", "parallel"),
        name="mla_attn_p",
    )(q, k, v)


def _mla_sample_kernel(q_ref, kp_ref, vp_ref, kn_ref, vn_ref, o_in_ref, o_ref, *, scale2):
    del o_in_ref
    q = q_ref[...]
    parts = [(_dot_nt(q, kp_ref[...]) * scale2, vp_ref[...]),
             (_dot_nt(q, kn_ref[...]) * scale2, vn_ref[...])]
    o_ref[...] = _softmax_pv(parts).astype(o_ref.dtype)


def _mla_sample_attn(q, k_past, v_past, k_all, v_all, o_prev, batch, t, past, heads, row0, scale):
    assert row0 % t == 0
    off = row0 // t
    new = lambda width: pl.BlockSpec((t, width), lambda b, h: (b + off, h))
    old = lambda width: pl.BlockSpec((past, width), lambda b, h: (b, h))
    return pl.pallas_call(
        functools.partial(_mla_sample_kernel, scale2=scale * LOG2E),
        grid=(batch, heads),
        in_specs=[new(2 * HEAD_DIM), old(2 * HEAD_DIM), old(HEAD_DIM), new(2 * HEAD_DIM), new(HEAD_DIM),
                  pl.BlockSpec(memory_space=pl.ANY)],
        out_specs=new(HEAD_DIM),
        out_shape=jax.ShapeDtypeStruct(o_prev.shape, o_prev.dtype),
        input_output_aliases={5: 0},
        compiler_params=_params("parallel", "parallel"),
        name="mla_attn_s",
    )(q, k_past, v_past, k_all, v_all, o_prev)


def _band_prompt_kernel(q_ref, k_ref, v_ref, bias_ref, o_ref, *, tq, scale2):
    t = q_ref.shape[0]
    band = bias_ref.shape[2]

    def scores(i, _):
        ke = (i + 1) * tq
        ks = max(0, ke - band)
        s = _dot_nt(q_ref[i * tq:(i + 1) * tq, :], k_ref[ks:ke, :])
        return s * scale2 + bias_ref[0, :, band - (ke - ks):]

    def finish(i, s):
        ke = (i + 1) * tq
        ks = max(0, ke - band)
        o_ref[i * tq:(i + 1) * tq, :] = _softmax_pv([(s, v_ref[ks:ke, :])]).astype(o_ref.dtype)

    _skewed(list(range(t // tq)), [scores, finish])


def _band_prompt_attn(qkv, bias2, batch, t, heads, scale):
    return pl.pallas_call(
        functools.partial(_band_prompt_kernel, tq=bias2.shape[1], scale2=scale * LOG2E),
        grid=(batch, heads),
        in_specs=[
            pl.BlockSpec((t, HEAD_DIM), lambda b, h: (b, h)),
            pl.BlockSpec((t, HEAD_DIM), lambda b, h: (b, h + heads)),
            pl.BlockSpec((t, HEAD_DIM), lambda b, h: (b, h + 2 * heads)),
            pl.BlockSpec((1,) + bias2.shape[1:], lambda b, h: (h, 0, 0)),
        ],
        out_specs=pl.BlockSpec((t, HEAD_DIM), lambda b, h: (b, h)),
        out_shape=jax.ShapeDtypeStruct((qkv.shape[0], heads * HEAD_DIM), BF16),
        compiler_params=_params("parallel", "parallel"),
        name="band_attn_p",
    )(qkv, qkv, qkv, bias2)


def _band_sample_kernel(q_ref, kp_ref, vp_ref, kn_ref, vn_ref, bias_ref, o_in_ref, o_ref, *, scale2):
    del o_in_ref
    win = kp_ref.shape[0]
    for hh in range(kp_ref.shape[1]):
        cols = slice(hh * HEAD_DIM, (hh + 1) * HEAD_DIM)
        q = q_ref[:, cols]
        parts = [(_dot_nt(q, kp_ref[:, hh, :].astype(BF16)) * scale2 + bias_ref[hh, :, :win],
                  vp_ref[:, hh, :].astype(BF16)),
                 (_dot_nt(q, kn_ref[:, cols]) * scale2 + bias_ref[hh, :, win:], vn_ref[:, cols])]
        o_ref[:, cols] = _softmax_pv(parts).astype(o_ref.dtype)


def _cache_view(cache):
    return cache.reshape(cache.shape[:2] + (-1, HEAD_DIM))


def _cache_spec(cache, g0):
    return pl.BlockSpec((None, cache.shape[1], HEAD_GROUP, HEAD_DIM), lambda b, hg: (b, 0, hg + g0, 0))


def _band_sample_attn(qkv, cache, bias, o_prev, batch, t, heads, row0, scale):
    assert row0 % t == 0
    off = row0 // t
    groups = heads // HEAD_GROUP
    new = lambda g0: pl.BlockSpec((t, HEAD_GROUP * HEAD_DIM), lambda b, hg: (b + off, hg + g0))
    return pl.pallas_call(
        functools.partial(_band_sample_kernel, scale2=scale * LOG2E),
        grid=(batch, groups),
        in_specs=[new(0), _cache_spec(cache, 0), _cache_spec(cache, groups), new(groups), new(2 * groups),
                  pl.BlockSpec((HEAD_GROUP,) + bias.shape[1:], lambda b, hg: (hg, 0, 0)),
                  pl.BlockSpec(memory_space=pl.ANY)],
        out_specs=new(0),
        out_shape=jax.ShapeDtypeStruct(o_prev.shape, o_prev.dtype),
        input_output_aliases={6: 0},
        compiler_params=_params("parallel", "parallel"),
        name="band_attn_s",
    )(qkv, _cache_view(cache), _cache_view(cache), qkv, qkv, bias, o_prev)


def _sb_items(n_q, tq, tk, lead):
    items = []
    for i in range(n_q):
        blocks = [(0, i * tq, tq, True)]
        end = i * tq
        while end > 0:
            w = min(tk, end)
            blocks.append((0, end - w, w, False))
            end -= w
        end = lead
        while end > 0:
            w = min(tk, end)
            blocks.append((1, end - w, w, False))
            end -= w
        for n, (src, k0, w, own) in enumerate(blocks):
            items.append((i, src, k0, w, own, n == 0, n == len(blocks) - 1))
    return items


def _sb_body(q_of, srcs, put, *, t, tq, tk, lead, scale2):
    tri_n = max(tq, tk)
    row = lax.broadcasted_iota(jnp.int32, (tri_n, tri_n), 0)
    col = lax.broadcasted_iota(jnp.int32, (tri_n, tri_n), 1)
    tri = jnp.where(row > col, 1.0, 0.0).astype(BF16)
    keep = (lax.broadcasted_iota(jnp.int32, (tq, tq), 1)
            < lax.broadcasted_iota(jnp.int32, (tq, tq), 0))
    state = {}

    def terms(item, _):
        i, src, k0, w, own, _, _ = item
        nz = _dot_nt(q_of(i), srcs[src][0](k0, w)) * (-scale2)
        neg_abs = lax.bitcast_convert_type(
            lax.bitcast_convert_type(nz, jnp.uint32) | jnp.uint32(0x80000000), F32)
        lf = jnp.minimum(nz, 0.0) - jnp.log(1.0 + jnp.exp2(neg_abs)) * LOG2E
        if own:
            lf = jnp.where(keep, lf, 0.0)
        hi = lf.astype(BF16)
        lo = (lf - hi.astype(F32)).astype(BF16)
        return nz, lf, (hi, lo), lf.sum(axis=-1, keepdims=True)

    def suffix(item, st):
        w = item[3]
        hi, lo = st[2]
        tri_w = tri[:w, :w]
        if w % LANES == 0:
            within = _dot(jnp.concatenate([hi, lo], axis=1), jnp.concatenate([tri_w, tri_w], axis=0))
        else:
            within = _dot(hi, tri_w) + _dot(lo, tri_w)
        return st + (within,)

    def weigh(item, st):
        i, src, k0, w, own, first, last = item
        nz, lf, _, total, within = st
        if first:
            state["carry"] = jnp.zeros((tq, 1), F32)
            state["acc"] = jnp.zeros((tq, HEAD_DIM), F32)
        a = jnp.exp2((lf - nz) + (within + state["carry"]))
        if own:
            a = jnp.where(keep, a, 0.0)
        state["acc"] = state["acc"] + _dot(a.astype(BF16), srcs[src][1](k0, w))
        state["carry"] = state["carry"] + total
        if last:
            put(i, state["acc"])

    _skewed(_sb_items(t // tq, tq, tk, lead), [terms, suffix, weigh])


def _rows_of(ref, c0=0):
    return lambda k0, w: ref[k0:k0 + w, c0:c0 + HEAD_DIM]


def _cached_rows_of(ref, hh):
    return lambda k0, w: ref[k0:k0 + w, hh, :].astype(BF16)


def _sb_prompt_kernel(q_ref, k_ref, v_ref, o_ref, *, tq, **kw):
    def put(i, acc):
        o_ref[i * tq:(i + 1) * tq, :] = acc.astype(o_ref.dtype)

    _sb_body(lambda i: q_ref[i * tq:(i + 1) * tq, :], [(_rows_of(k_ref), _rows_of(v_ref))], put,
             t=q_ref.shape[0], tq=tq, lead=0, **kw)


def _sb_sample_kernel(q_ref, kn_ref, vn_ref, kp_ref, vp_ref, o_in_ref, o_ref, **kw):
    del o_in_ref
    t = q_ref.shape[0]
    for hh in range(kp_ref.shape[1]):
        c0 = hh * HEAD_DIM

        def put(i, acc, c0=c0):
            o_ref[:, c0:c0 + HEAD_DIM] = acc.astype(o_ref.dtype)

        _sb_body(lambda i, c0=c0: q_ref[:, c0:c0 + HEAD_DIM],
                 [(_rows_of(kn_ref, c0), _rows_of(vn_ref, c0)),
                  (_cached_rows_of(kp_ref, hh), _cached_rows_of(vp_ref, hh))],
                 put, t=t, tq=t, lead=kp_ref.shape[0], **kw)


def _sb_prompt_attn(qkv, batch, t, heads, scale):
    tq = _pick(t, (256, 128, 64))
    return pl.pallas_call(
        functools.partial(_sb_prompt_kernel, tq=tq, tk=MXU_DIM, scale2=scale * LOG2E),
        grid=(batch, heads),
        in_specs=[
            pl.BlockSpec((t, HEAD_DIM), lambda b, h: (b, h)),
            pl.BlockSpec((t, HEAD_DIM), lambda b, h: (b, h + heads)),
            pl.BlockSpec((t, HEAD_DIM), lambda b, h: (b, h + 2 * heads)),
        ],
        out_specs=pl.BlockSpec((t, HEAD_DIM), lambda b, h: (b, h)),
        out_shape=jax.ShapeDtypeStruct((qkv.shape[0], heads * HEAD_DIM), BF16),
        compiler_params=_params("parallel", "parallel"),
        name="sb_attn_p",
    )(qkv, qkv, qkv)


def _sb_sample_attn(qkv, cache, o_prev, batch, t, heads, row0, scale):
    assert row0 % t == 0
    off = row0 // t
    groups = heads // HEAD_GROUP
    new = lambda g0: pl.BlockSpec((t, HEAD_GROUP * HEAD_DIM), lambda b, hg: (b + off, hg + g0))
    return pl.pallas_call(
        functools.partial(_sb_sample_kernel, tk=MXU_DIM, scale2=scale * LOG2E),
        grid=(batch, groups),
        in_specs=[new(0), new(groups), new(2 * groups), _cache_spec(cache, 0), _cache_spec(cache, groups),
                  pl.BlockSpec(memory_space=pl.ANY)],
        out_specs=new(0),
        out_shape=jax.ShapeDtypeStruct(o_prev.shape, o_prev.dtype),
        input_output_aliases={5: 0},
        compiler_params=_params("parallel", "parallel"),
        name="sb_attn_s",
    )(qkv, qkv, qkv, _cache_view(cache), _cache_view(cache), o_prev)


def _rope_tables(pos):
    half = ROPE_DIM // 2
    inv_freq = ROPE_THETA ** (-jnp.arange(half, dtype=F32) / half)
    ang = pos.astype(F32)[:, None] * inv_freq[None, :]
    cos = jnp.cos(ang)
    sin = jnp.sin(ang)
    zero = jnp.zeros((pos.shape[0], LANES - ROPE_DIM), F32)
    return jnp.concatenate([cos, cos, zero], axis=1), jnp.concatenate([sin, sin, zero], axis=1)


def _rotate_cols(w):
    half = ROPE_DIM // 2
    return jnp.concatenate([-w[..., half:], w[..., :half]], axis=-1)


def _mla_layer(x, g, cosz, sinz, past_ckv, past_krope, w_down, g_q, w_uq, g_kv, w_uk, w_uv, w_o, dims):
    bp, tp, bs, ts = dims
    np_rows = bp * tp
    q_lora = g_q.shape[0]
    kv_lora = g_kv.shape[0]
    heads = w_uk.shape[1]
    past = past_ckv.shape[1]
    scale = (HEAD_DIM + ROPE_DIM) ** -0.5

    w_rope = w_down[:, q_lora + kv_lora:]
    w_down_ext = jnp.concatenate([w_down, _rotate_cols(w_rope)], axis=1).astype(BF16)
    wq = w_uq.reshape(q_lora, heads, HEAD_DIM + ROPE_DIM)
    w_uq_ext = jnp.concatenate([wq, _rotate_cols(wq[..., HEAD_DIM:])], axis=-1)
    w_uq_ext = w_uq_ext.reshape(q_lora, heads * 2 * HEAD_DIM).astype(BF16)
    w_uk2 = w_uk.reshape(kv_lora, heads * HEAD_DIM).astype(BF16)
    w_uv2 = w_uv.reshape(kv_lora, heads * HEAD_DIM).astype(BF16)

    qn, ckv, ckv_b, kr, kr_b = _mla_down(x, g, w_down_ext, g_q, g_kv, cosz, sinz, q_lora, kv_lora)
    q = _mla_q(qn, w_uq_ext, cosz, sinz)
    k_all, v_all = _mla_kv(ckv_b, w_uk2, w_uv2, kr_b)
    kr_past = jnp.pad(past_krope.reshape(bs * past, ROPE_DIM), ((0, 0), (0, LANES - ROPE_DIM)))
    k_past, v_past = _mla_kv(past_ckv.reshape(bs * past, kv_lora), w_uk2, w_uv2, kr_past)

    o = _mla_prompt_attn(q, k_all, v_all, bp, tp, heads, scale)
    o = _mla_sample_attn(q, k_past, v_past, k_all, v_all, o, bs, ts, past, heads, np_rows, scale)
    x = _mm_res(o, w_o.astype(BF16), x)
    outs = (ckv[:np_rows].reshape(bp, tp, kv_lora), kr[:np_rows].reshape(bp, tp, ROPE_DIM),
            ckv[np_rows:].reshape(bs, ts, kv_lora), kr[np_rows:].reshape(bs, ts, ROPE_DIM))
    return x, outs


def _band_bias(rel_bias, nch):
    band = LEFT_CTX + CHUNK
    far = band - 1 - REL_CLIP
    u = jnp.concatenate([jnp.repeat(rel_bias[:, 2 * REL_CLIP:], far, axis=1),
                         rel_bias[:, REL_CLIP - (CHUNK - 1):2 * REL_CLIP + 1][:, ::-1]], axis=1) * LOG2E
    bias = jnp.stack([u[:, CHUNK - 1 - q:CHUNK - 1 - q + band] for q in range(CHUNK)], axis=1).astype(F32)
    rows = [jnp.pad(bias, ((0, 0), (0, 0), (c * CHUNK, (nch - 1 - c) * CHUNK)), constant_values=NEG_INF)
            for c in range(nch)]
    return jnp.concatenate(rows, axis=1)


def _band_layer(x, g, cache, w_qkv, rel_bias, w_o, dims):
    bp, tp, bs, ts = dims
    np_rows = bp * tp
    heads = rel_bias.shape[0]
    hd = heads * HEAD_DIM
    win = cache.shape[1]
    scale = HEAD_DIM ** -0.5

    w = w_qkv.astype(BF16)
    qkv_b, kv_p = _qkv(x, g, w, hd, 0, np_rows)
    qkv_b, kv_s = _qkv(x, g, w, hd, np_rows, bs * ts, qkv_b)
    o = _band_prompt_attn(qkv_b, _band_bias(rel_bias, 2), bp, tp, heads, scale)
    o = _band_sample_attn(qkv_b, cache, _band_bias(rel_bias, 1), o, bs, ts, heads, np_rows, scale)
    x = _mm_res(o, w_o.astype(BF16), x)

    wp = min(LEFT_CTX, tp)
    st_p = kv_p.reshape(bp, tp, 2, heads, HEAD_DIM)[:, tp - wp:]
    new = kv_s.reshape(bs, ts, 2, heads, HEAD_DIM)
    st_s = jnp.concatenate([cache, new], axis=1)[:, ts:]
    return x, (st_p, st_s)


def _sb_layer(x, g, cache, w_qkv, w_o, dims):
    bp, tp, bs, ts = dims
    np_rows = bp * tp
    hd = w_o.shape[0]
    heads = hd // HEAD_DIM
    scale = HEAD_DIM ** -0.5

    w = w_qkv.astype(BF16)
    qkv_b, kv_p = _qkv(x, g, w, hd, 0, np_rows)
    qkv_b, kv_s = _qkv(x, g, w, hd, np_rows, bs * ts, qkv_b)
    o = _sb_prompt_attn(qkv_b, bp, tp, heads, scale)
    o = _sb_sample_attn(qkv_b, cache, o, bs, ts, heads, np_rows, scale)
    x = _mm_res(o, w_o.astype(BF16), x)
    st_p = kv_p.reshape(bp, tp, 2, heads, HEAD_DIM)
    st_s = kv_s.reshape(bs, ts, 2, heads, HEAD_DIM)
    return x, (st_p, st_s)


def kernel(x_prompt, x_sample, cache_mla_ckv, cache_mla_krope, cache_band_kv, cache_sb_kv, norm_g, final_norm_g, ffn_w_gu, ffn_w_down, mla_w_down, mla_q_norm_g, mla_kv_norm_g, mla_w_uq, mla_w_uk, mla_w_uv, mla_w_o, band_w_qkv, band_rel_bias, band_w_o, sb_w_qkv, sb_w_o):
    bp, tp, d = x_prompt.shape
    bs, ts, _ = x_sample.shape
    depth = norm_g.shape[0]
    past = cache_mla_ckv.shape[2]
    dims = (bp, tp, bs, ts)
    np_rows = bp * tp
    ns_rows = bs * ts

    x = jnp.concatenate([x_prompt.reshape(np_rows, d), x_sample.reshape(ns_rows, d)], axis=0)
    w_gu = ffn_w_gu[0, 0].astype(BF16)
    w_dn = ffn_w_down[0, 0].astype(BF16)

    def ffn(x, g, w_gu, w_dn, k):
        if k + 1 == 2 * depth:
            return _ffn(x, g, w_gu, w_dn), None, None
        return _ffn(x, g, w_gu, w_dn, (ffn_w_gu, ffn_w_down, (k + 1) // 2, (k + 1) % 2))

    pos = jnp.concatenate([jnp.tile(jnp.arange(tp), bp), jnp.tile(past + jnp.arange(ts), bs)])
    cosz, sinz = _rope_tables(pos)

    ckv_p, krope_p, band_p, sb_p = [], [], [], []
    ckv_s, krope_s, band_s, sb_s = [], [], [], []
    for i in range(depth):
        kind, j = i % 3, i // 3
        x, w_gu, w_dn = ffn(x, norm_g[i, 0], w_gu, w_dn, 2 * i)
        if kind == 0:
            x, (c_p, k_p, c_s, k_s) = _mla_layer(
                x, norm_g[i, 1], cosz, sinz, cache_mla_ckv[j], cache_mla_krope[j],
                mla_w_down[j], mla_q_norm_g[j], mla_w_uq[j], mla_kv_norm_g[j], mla_w_uk[j], mla_w_uv[j],
                mla_w_o[j], dims)
            ckv_p.append(c_p); krope_p.append(k_p); ckv_s.append(c_s); krope_s.append(k_s)
        elif kind == 1:
            x, (st_p, st_s) = _band_layer(x, norm_g[i, 1], cache_band_kv[j], band_w_qkv[j], band_rel_bias[j],
                                          band_w_o[j], dims)
            band_p.append(st_p); band_s.append(st_s)
        else:
            x, (st_p, st_s) = _sb_layer(x, norm_g[i, 1], cache_sb_kv[j], sb_w_qkv[j], sb_w_o[j], dims)
            sb_p.append(st_p); sb_s.append(st_s)
        x, w_gu, w_dn = ffn(x, norm_g[i, 2], w_gu, w_dn, 2 * i + 1)

    y_prompt = _final_norm(x, final_norm_g, 0, np_rows).reshape(bp, tp, d)
    y_sample = _final_norm(x, final_norm_g, np_rows, ns_rows).reshape(bs, ts, d)
    return (y_prompt, y_sample,
            jnp.stack(ckv_p), jnp.stack(krope_p), jnp.stack(band_p), jnp.stack(sb_p),
            jnp.stack(ckv_s), jnp.stack(krope_s), jnp.stack(band_s), jnp.stack(sb_s))
```

```python
import functools
import math

import jax
import jax.numpy as jnp
from jax import lax
from jax.experimental import pallas as pl
from jax.experimental.pallas import tpu as pltpu

F32 = jnp.float32
BF16 = jnp.bfloat16

EPS = 1e-6
NEG_INF = -1e30
CHUNK = 64
HEAD_DIM = 128
ROPE_DIM = 64
ROPE_THETA = 10000.0
LEFT_CHUNKS = 8
LEFT_CTX = LEFT_CHUNKS * CHUNK
REL_CLIP = 128
LANES = 128
MXU_DIM = 256
HEAD_GROUP = 8
HEADS_PER_STEP = 2
LOG2E = math.log2(math.e)
VMEM_LIMIT = 56 * 1024 * 1024
FFN_VMEM_LIMIT = 60 * 1024 * 1024


def _pick(n, candidates):
    for c in candidates:
        if n % c == 0:
            return c
    raise ValueError(f"no tile in {candidates} divides {n}")


def _params(*sem, vmem=VMEM_LIMIT):
    return pltpu.CompilerParams(dimension_semantics=sem, vmem_limit_bytes=vmem)


def _rms(x, g):
    return x * lax.rsqrt(jnp.mean(x * x, axis=-1, keepdims=True) + EPS) * g


def _dot(a, b):
    return jnp.dot(a, b, preferred_element_type=F32)


def _dot_nt(a, b):
    return lax.dot_general(a, b, (((1,), (1,)), ((), ())), preferred_element_type=F32)


def _skewed(items, stages):
    n, ns = len(items), len(stages)
    live = {}
    for step in range(n + ns - 1):
        for s in range(ns):
            t = step - s
            if 0 <= t < n:
                live[t] = stages[s](items[t], live.get(t))
        live.pop(step - ns + 1, None)


def _ffn_kernel(*refs, n_col, cast_next):
    x_ref, g_ref, wg_ref, wu_ref, wd_ref = refs[:5]
    if cast_next:
        ngu_ref, ndn_ref, o_ref, ngu_out, ndn_out, h_ref = refs[5:]
    else:
        o_ref, h_ref = refs[5:]

    def chunk(first):
        h = h_ref[...]
        gate = _dot(h, wg_ref[...])
        if cast_next:
            ngu_out[...] = ngu_ref[...].astype(BF16)
            ndn_out[...] = ndn_ref[...].astype(BF16)
        up = _dot(h, wu_ref[...])
        a = (gate * jax.nn.sigmoid(gate) * up).astype(BF16)
        base = x_ref if first else o_ref
        for n in range(0, o_ref.shape[1], n_col):
            o_ref[:, n:n + n_col] = base[:, n:n + n_col] + 0.5 * _dot(a, wd_ref[:, n:n + n_col])

    @pl.when(pl.program_id(1) == 0)
    def _():
        h_ref[...] = _rms(x_ref[...], g_ref[...]).astype(BF16)
        chunk(True)

    @pl.when(pl.program_id(1) > 0)
    def _():
        chunk(False)


def _ffn(x, g, w_gu, w_down, nxt=None):
    n, d = x.shape
    d_ff = w_down.shape[0]
    tm = _pick(n, (512, 256, 128))
    tf = _pick(d_ff, (256, 128))
    ni, nf = n // tm, d_ff // tf
    in_specs = [
        pl.BlockSpec((tm, d), lambda i, j: (i, 0)),
        pl.BlockSpec((1, d), lambda i, j: (0, 0)),
        pl.BlockSpec((d, tf), lambda i, j: (0, j)),
        pl.BlockSpec((d, tf), lambda i, j: (0, j + nf)),
        pl.BlockSpec((tf, d), lambda i, j: (j, 0)),
    ]
    out_specs = [pl.BlockSpec((tm, d), lambda i, j: (i, 0))]
    out_shape = [jax.ShapeDtypeStruct((n, d), F32)]
    args = [x, g.reshape(1, d), w_gu, w_gu, w_down]
    if nxt is not None:
        next_gu, next_dn, layer, half = nxt
        cr = min(1 << (ni.bit_length() - 1), d // LANES)
        rows, cols = d // cr, 2 * tf
        slab_i = lambda i: jnp.minimum(i, cr - 1)
        slab_j = lambda i, j: jnp.where(i < cr, j, nf - 1)
        in_specs += [
            pl.BlockSpec((None, None, rows, cols), lambda i, j: (layer, half, slab_i(i), slab_j(i, j))),
            pl.BlockSpec((None, None, tf, rows), lambda i, j: (layer, half, slab_j(i, j), slab_i(i))),
        ]
        out_specs += [
            pl.BlockSpec((rows, cols), lambda i, j: (slab_i(i), slab_j(i, j))),
            pl.BlockSpec((tf, rows), lambda i, j: (slab_j(i, j), slab_i(i))),
        ]
        out_shape += [jax.ShapeDtypeStruct((d, 2 * d_ff), BF16), jax.ShapeDtypeStruct((d_ff, d), BF16)]
        args += [next_gu, next_dn]
    out = pl.pallas_call(
        functools.partial(_ffn_kernel, n_col=_pick(d, (512, 256, 128)), cast_next=nxt is not None),
        grid=(ni, nf),
        in_specs=in_specs,
        out_specs=out_specs,
        out_shape=out_shape,
        scratch_shapes=[pltpu.VMEM((tm, d), BF16)],
        compiler_params=_params("arbitrary", "arbitrary", vmem=FFN_VMEM_LIMIT),
        name="ffn",
    )(*args)
    return out if nxt is not None else out[0]


def _qkv_kernel(*refs, aliased):
    x_ref, g_ref, w_ref = refs[:3]
    ob_ref, of_ref, h_ref = refs[4:] if aliased else refs[3:]
    j = pl.program_id(1)

    @pl.when(j == 0)
    def _():
        h_ref[...] = _rms(x_ref[...], g_ref[...]).astype(BF16)

    y = _dot(h_ref[...], w_ref[...])
    ob_ref[...] = y.astype(BF16)
    of_ref[...] = y.reshape(of_ref.shape)


def _qkv(x, g, w, q_cols, row0, rows, qkv_prev=None):
    n, d = x.shape
    nout = w.shape[1]
    tm = _pick(rows, (512, 256, 128))
    tn = _pick(math.gcd(q_cols, nout), (1024, 512, 256, 128))
    qb = q_cols // tn
    assert row0 % tm == 0
    off = row0 // tm
    in_specs = [
        pl.BlockSpec((tm, d), lambda i, j: (i + off, 0)),
        pl.BlockSpec((1, d), lambda i, j: (0, 0)),
        pl.BlockSpec((d, tn), lambda i, j: (0, j)),
    ]
    args = [x, g.reshape(1, d), w]
    if qkv_prev is not None:
        in_specs.append(pl.BlockSpec(memory_space=pl.ANY))
        args.append(qkv_prev)
    return pl.pallas_call(
        functools.partial(_qkv_kernel, aliased=qkv_prev is not None),
        grid=(rows // tm, nout // tn),
        in_specs=in_specs,
        out_specs=[
            pl.BlockSpec((tm, tn), lambda i, j: (i + off, j)),
            pl.BlockSpec((tm, tn // HEAD_DIM, HEAD_DIM), lambda i, j: (i, jnp.maximum(j - qb, 0), 0)),
        ],
        out_shape=[
            jax.ShapeDtypeStruct((n, nout), BF16),
            jax.ShapeDtypeStruct((rows, (nout - q_cols) // HEAD_DIM, HEAD_DIM), F32),
        ],
        input_output_aliases={3: 0} if qkv_prev is not None else {},
        scratch_shapes=[pltpu.VMEM((tm, d), BF16)],
        compiler_params=_params("parallel", "arbitrary"),
        name="qkv",
    )(*args)


def _mm_res_kernel(a_ref, w_ref, r_ref, o_ref):
    o_ref[...] = r_ref[...] + _dot(a_ref[...], w_ref[...])


def _mm_res(a, w, res):
    n, k = a.shape
    nout = w.shape[1]
    tm = _pick(n, (512, 256, 128))
    tn = _pick(nout, (1024, 512, 256, 128))
    return pl.pallas_call(
        _mm_res_kernel,
        grid=(n // tm, nout // tn),
        in_specs=[
            pl.BlockSpec((tm, k), lambda i, j: (i, 0)),
            pl.BlockSpec((k, tn), lambda i, j: (0, j)),
            pl.BlockSpec((tm, tn), lambda i, j: (i, j)),
        ],
        out_specs=pl.BlockSpec((tm, tn), lambda i, j: (i, j)),
        out_shape=jax.ShapeDtypeStruct((n, nout), F32),
        compiler_params=_params("parallel", "parallel"),
        name="mm_res",
    )(a, w, res)


def _final_norm_kernel(x_ref, g_ref, o_ref):
    o_ref[...] = _rms(x_ref[...], g_ref[...])


def _final_norm(x, g, row0, rows):
    d = x.shape[1]
    tm = _pick(rows, (512, 256, 128))
    assert row0 % tm == 0
    off = row0 // tm
    return pl.pallas_call(
        _final_norm_kernel,
        grid=(rows // tm,),
        in_specs=[
            pl.BlockSpec((tm, d), lambda i: (i + off, 0)),
            pl.BlockSpec((1, d), lambda i: (0, 0)),
        ],
        out_specs=pl.BlockSpec((tm, d), lambda i: (i, 0)),
        out_shape=jax.ShapeDtypeStruct((rows, d), F32),
        compiler_params=_params("parallel"),
        name="final_norm",
    )(x, g.reshape(1, d))


def _rope_group(r, cosz, sinz):
    return r * cosz + pltpu.roll(r, ROPE_DIM, axis=1) * sinz


def _mla_down_kernel(x_ref, g_ref, w_ref, gq_ref, gkv_ref, cosz_ref, sinz_ref,
                     qn_ref, ckv_ref, ckvb_ref, kr_ref, krb_ref, *, q_lora, kv_lora):
    h = _rms(x_ref[...], g_ref[...]).astype(BF16)
    y = _dot(h, w_ref[...])
    qn_ref[...] = _rms(y[:, :q_lora], gq_ref[...]).astype(BF16)
    ckv = _rms(y[:, q_lora:q_lora + kv_lora], gkv_ref[...])
    ckv_ref[...] = ckv
    ckvb_ref[...] = ckv.astype(BF16)
    r0 = q_lora + kv_lora
    kr = _rope_group(y[:, r0:r0 + 2 * ROPE_DIM], cosz_ref[...], sinz_ref[...])
    kr_ref[...] = kr[:, :ROPE_DIM]
    krb_ref[...] = kr.astype(BF16)


def _mla_down(x, g, w_ext, gq, gkv, cosz, sinz, q_lora, kv_lora):
    n, d = x.shape
    wn = w_ext.shape[1]
    tm = _pick(n, (256, 128))
    row = lambda i: (i, 0)
    fix = lambda i: (0, 0)
    return pl.pallas_call(
        functools.partial(_mla_down_kernel, q_lora=q_lora, kv_lora=kv_lora),
        grid=(n // tm,),
        in_specs=[
            pl.BlockSpec((tm, d), row),
            pl.BlockSpec((1, d), fix),
            pl.BlockSpec((d, wn), fix),
            pl.BlockSpec((1, q_lora), fix),
            pl.BlockSpec((1, kv_lora), fix),
            pl.BlockSpec((tm, LANES), row),
            pl.BlockSpec((tm, LANES), row),
        ],
        out_specs=[
            pl.BlockSpec((tm, q_lora), row),
            pl.BlockSpec((tm, kv_lora), row),
            pl.BlockSpec((tm, kv_lora), row),
            pl.BlockSpec((tm, ROPE_DIM), row),
            pl.BlockSpec((tm, 2 * ROPE_DIM), row),
        ],
        out_shape=[
            jax.ShapeDtypeStruct((n, q_lora), BF16),
            jax.ShapeDtypeStruct((n, kv_lora), F32),
            jax.ShapeDtypeStruct((n, kv_lora), BF16),
            jax.ShapeDtypeStruct((n, ROPE_DIM), F32),
            jax.ShapeDtypeStruct((n, 2 * ROPE_DIM), BF16),
        ],
        compiler_params=_params("parallel"),
        name="mla_down",
    )(x, g.reshape(1, d), w_ext, gq.reshape(1, -1), gkv.reshape(1, -1), cosz, sinz)


def _mla_q_kernel(a_ref, w_ref, cosz_ref, sinz_ref, o_ref):
    y = _dot(a_ref[...], w_ref[...])
    cosz = cosz_ref[...]
    sinz = sinz_ref[...]
    for c in range(0, y.shape[1], 2 * HEAD_DIM):
        rot = _rope_group(y[:, c + HEAD_DIM:c + 2 * HEAD_DIM], cosz, sinz)
        o_ref[:, c:c + HEAD_DIM] = y[:, c:c + HEAD_DIM].astype(BF16)
        o_ref[:, c + HEAD_DIM:c + 2 * HEAD_DIM] = rot.astype(BF16)


def _mla_q(qn, w_ext, cosz, sinz):
    n, k = qn.shape
    nout = w_ext.shape[1]
    tm = _pick(n, (512, 256, 128))
    tn = _pick(nout, (2048, 1024, 512, 256))
    return pl.pallas_call(
        _mla_q_kernel,
        grid=(n // tm, nout // tn),
        in_specs=[
            pl.BlockSpec((tm, k), lambda i, j: (i, 0)),
            pl.BlockSpec((k, tn), lambda i, j: (0, j)),
            pl.BlockSpec((tm, LANES), lambda i, j: (i, 0)),
            pl.BlockSpec((tm, LANES), lambda i, j: (i, 0)),
        ],
        out_specs=pl.BlockSpec((tm, tn), lambda i, j: (i, j)),
        out_shape=jax.ShapeDtypeStruct((n, nout), BF16),
        compiler_params=_params("parallel", "parallel"),
        name="mla_q",
    )(qn, w_ext, cosz, sinz)


def _mla_kv_kernel(c_ref, wk_ref, wv_ref, kr_ref, k_ref, v_ref):
    c = c_ref[...].astype(BF16)
    yk = _dot(c, wk_ref[...]).astype(BF16)
    v_ref[...] = _dot(c, wv_ref[...]).astype(BF16)
    kr = kr_ref[...].astype(BF16)
    for hh in range(yk.shape[1] // HEAD_DIM):
        k_ref[:, 2 * hh * HEAD_DIM:(2 * hh + 1) * HEAD_DIM] = yk[:, hh * HEAD_DIM:(hh + 1) * HEAD_DIM]
        k_ref[:, (2 * hh + 1) * HEAD_DIM:(2 * hh + 2) * HEAD_DIM] = kr


def _mla_kv(c, w_uk, w_uv, kr):
    n, k = c.shape
    hd = w_uk.shape[1]
    tm = _pick(n, (512, 256, 128))
    tn = _pick(hd, (2048, 1024, 512, 256, 128))
    return pl.pallas_call(
        _mla_kv_kernel,
        grid=(n // tm, hd // tn),
        in_specs=[
            pl.BlockSpec((tm, k), lambda i, j: (i, 0)),
            pl.BlockSpec((k, tn), lambda i, j: (0, j)),
            pl.BlockSpec((k, tn), lambda i, j: (0, j)),
            pl.BlockSpec((tm, LANES), lambda i, j: (i, 0)),
        ],
        out_specs=[
            pl.BlockSpec((tm, 2 * tn), lambda i, j: (i, j)),
            pl.BlockSpec((tm, tn), lambda i, j: (i, j)),
        ],
        out_shape=[
            jax.ShapeDtypeStruct((n, 2 * hd), BF16),
            jax.ShapeDtypeStruct((n, hd), BF16),
        ],
        compiler_params=_params("parallel", "parallel"),
        name="mla_kv",
    )(c, w_uk, w_uv, kr)


def _softmax_pv(parts):
    m = parts[0][0].max(axis=-1, keepdims=True)
    for t, _ in parts[1:]:
        m = jnp.maximum(m, t.max(axis=-1, keepdims=True))
    l = None
    acc = None
    for t, v in parts:
        p = jnp.exp2(t - m)
        ps = p.sum(axis=-1, keepdims=True)
        pv = _dot(p.astype(BF16), v)
        l = ps if l is None else l + ps
        acc = pv if acc is None else acc + pv
    return acc / l


def _interleave(n_streams, n_items):
    return [(hh, i) for i in range(n_items) for hh in range(n_streams)]


def _mla_prompt_kernel(q_ref, k_ref, v_ref, o_ref, *, tq, scale2):
    t = q_ref.shape[0]
    r = lax.broadcasted_iota(jnp.int32, (tq, tq), 0)
    c = lax.broadcasted_iota(jnp.int32, (tq, tq), 1)
    diag = c < (r | (CHUNK - 1)) + 1

    def scores(item, _):
        hh, i = item
        qk = slice(hh * 2 * HEAD_DIM, (hh + 1) * 2 * HEAD_DIM)
        return _dot_nt(q_ref[i * tq:(i + 1) * tq, qk], k_ref[0:(i + 1) * tq, qk]) * scale2

    def finish(item, s):
        hh, i = item
        vo = slice(hh * HEAD_DIM, (hh + 1) * HEAD_DIM)
        lo = i * tq
        parts = [(s[:, :lo], v_ref[0:lo, vo])] if i else []
        parts.append((jnp.where(diag, s[:, lo:], NEG_INF), v_ref[lo:lo + tq, vo]))
        o_ref[lo:lo + tq, vo] = _softmax_pv(parts).astype(o_ref.dtype)

    _skewed(_interleave(o_ref.shape[1] // HEAD_DIM, t // tq), [scores, finish])


def _mla_prompt_attn(q, k, v, batch, t, heads, scale):
    tq = _pick(t, (256, 128))
    hps = HEADS_PER_STEP
    return pl.pallas_call(
        functools.partial(_mla_prompt_kernel, tq=tq, scale2=scale * LOG2E),
        grid=(batch, heads // hps),
        in_specs=[
            pl.BlockSpec((t, hps * 2 * HEAD_DIM), lambda b, h: (b, h)),
            pl.BlockSpec((t, hps * 2 * HEAD_DIM), lambda b, h: (b, h)),
            pl.BlockSpec((t, hps * HEAD_DIM), lambda b, h: (b, h)),
        ],
        out_specs=pl.BlockSpec((t, hps * HEAD_DIM), lambda b, h: (b, h)),
        out_shape=jax.ShapeDtypeStruct((q.shape[0], heads * HEAD_DIM), BF16),
        compiler_params=_params("parallel", "parallel"),
        name="mla_attn_p",
    )(q, k, v)


def _mla_sample_kernel(q_ref, kp_ref, vp_ref, kn_ref, vn_ref, o_in_ref, o_ref, *, scale2):
    del o_in_ref
    for hh in range(o_ref.shape[1] // HEAD_DIM):
        qk = slice(hh * 2 * HEAD_DIM, (hh + 1) * 2 * HEAD_DIM)
        vo = slice(hh * HEAD_DIM, (hh + 1) * HEAD_DIM)
        q = q_ref[:, qk]
        parts = [(_dot_nt(q, kp_ref[:, qk]) * scale2, vp_ref[:, vo]),
                 (_dot_nt(q, kn_ref[:, qk]) * scale2, vn_ref[:, vo])]
        o_ref[:, vo] = _softmax_pv(parts).astype(o_ref.dtype)


def _mla_sample_attn(q, k_past, v_past, k_all, v_all, o_prev, batch, t, past, heads, row0, scale):
    assert row0 % t == 0
    off = row0 // t
    hps = _pick(heads, (4, 2, 1))
    new = lambda width: pl.BlockSpec((t, hps * width), lambda b, h: (b + off, h))
    old = lambda width: pl.BlockSpec((past, hps * width), lambda b, h: (b, h))
    return pl.pallas_call(
        functools.partial(_mla_sample_kernel, scale2=scale * LOG2E),
        grid=(batch, heads // hps),
        in_specs=[new(2 * HEAD_DIM), old(2 * HEAD_DIM), old(HEAD_DIM), new(2 * HEAD_DIM), new(HEAD_DIM),
                  pl.BlockSpec(memory_space=pl.ANY)],
        out_specs=new(HEAD_DIM),
        out_shape=jax.ShapeDtypeStruct(o_prev.shape, o_prev.dtype),
        input_output_aliases={5: 0},
        compiler_params=_params("parallel", "parallel"),
        name="mla_attn_s",
    )(q, k_past, v_past, k_all, v_all, o_prev)


def _band_prompt_kernel(q_ref, k_ref, v_ref, bias_ref, o_ref, *, tq, scale2):
    t = q_ref.shape[0]
    band = bias_ref.shape[2]

    def scores(item, _):
        hh, i = item
        cols = slice(hh * HEAD_DIM, (hh + 1) * HEAD_DIM)
        ke = (i + 1) * tq
        ks = max(0, ke - band)
        s = _dot_nt(q_ref[i * tq:(i + 1) * tq, cols], k_ref[ks:ke, cols])
        return s * scale2 + bias_ref[hh, :, band - (ke - ks):]

    def finish(item, s):
        hh, i = item
        cols = slice(hh * HEAD_DIM, (hh + 1) * HEAD_DIM)
        ke = (i + 1) * tq
        ks = max(0, ke - band)
        o_ref[i * tq:(i + 1) * tq, cols] = _softmax_pv([(s, v_ref[ks:ke, cols])]).astype(o_ref.dtype)

    _skewed(_interleave(bias_ref.shape[0], t // tq), [scores, finish])


def _band_prompt_attn(qkv, bias2, batch, t, heads, scale):
    hps = HEADS_PER_STEP
    groups = heads // hps
    spec = lambda g0: pl.BlockSpec((t, hps * HEAD_DIM), lambda b, h: (b, h + g0))
    return pl.pallas_call(
        functools.partial(_band_prompt_kernel, tq=bias2.shape[1], scale2=scale * LOG2E),
        grid=(batch, groups),
        in_specs=[spec(0), spec(groups), spec(2 * groups),
                  pl.BlockSpec((hps,) + bias2.shape[1:], lambda b, h: (h, 0, 0))],
        out_specs=spec(0),
        out_shape=jax.ShapeDtypeStruct((qkv.shape[0], heads * HEAD_DIM), BF16),
        compiler_params=_params("parallel", "parallel"),
        name="band_attn_p",
    )(qkv, qkv, qkv, bias2)


def _band_sample_kernel(q_ref, kp_ref, vp_ref, kn_ref, vn_ref, bias_ref, o_in_ref, o_ref, *, scale2):
    del o_in_ref
    win = kp_ref.shape[0]
    for hh in range(kp_ref.shape[1]):
        cols = slice(hh * HEAD_DIM, (hh + 1) * HEAD_DIM)
        q = q_ref[:, cols]
        parts = [(_dot_nt(q, kp_ref[:, hh, :].astype(BF16)) * scale2 + bias_ref[hh, :, :win],
                  vp_ref[:, hh, :].astype(BF16)),
                 (_dot_nt(q, kn_ref[:, cols]) * scale2 + bias_ref[hh, :, win:], vn_ref[:, cols])]
        o_ref[:, cols] = _softmax_pv(parts).astype(o_ref.dtype)


def _cache_view(cache):
    return cache.reshape(cache.shape[:2] + (-1, HEAD_DIM))


def _cache_spec(cache, g0):
    return pl.BlockSpec((None, cache.shape[1], HEAD_GROUP, HEAD_DIM), lambda b, hg: (b, 0, hg + g0, 0))


def _band_sample_attn(qkv, cache, bias, o_prev, batch, t, heads, row0, scale):
    assert row0 % t == 0
    off = row0 // t
    groups = heads // HEAD_GROUP
    new = lambda g0: pl.BlockSpec((t, HEAD_GROUP * HEAD_DIM), lambda b, hg: (b + off, hg + g0))
    return pl.pallas_call(
        functools.partial(_band_sample_kernel, scale2=scale * LOG2E),
        grid=(batch, groups),
        in_specs=[new(0), _cache_spec(cache, 0), _cache_spec(cache, groups), new(groups), new(2 * groups),
                  pl.BlockSpec((HEAD_GROUP,) + bias.shape[1:], lambda b, hg: (hg, 0, 0)),
                  pl.BlockSpec(memory_space=pl.ANY)],
        out_specs=new(0),
        out_shape=jax.ShapeDtypeStruct(o_prev.shape, o_prev.dtype),
        input_output_aliases={6: 0},
        compiler_params=_params("parallel", "parallel"),
        name="band_attn_s",
    )(qkv, _cache_view(cache), _cache_view(cache), qkv, qkv, bias, o_prev)


def _sb_items(n_q, tq, tk, lead):
    items = []
    for i in range(n_q):
        blocks = [(0, i * tq, tq, True)]
        end = i * tq
        while end > 0:
            w = min(tk, end)
            blocks.append((0, end - w, w, False))
            end -= w
        end = lead
        while end > 0:
            w = min(tk, end)
            blocks.append((1, end - w, w, False))
            end -= w
        for n, (src, k0, w, own) in enumerate(blocks):
            items.append((i, src, k0, w, own, n == 0, n == len(blocks) - 1))
    return items


def _sb_body(streams, *, t, tq, tk, lead, scale2):
    tri_n = max(tq, tk)
    row = lax.broadcasted_iota(jnp.int32, (tri_n, tri_n), 0)
    col = lax.broadcasted_iota(jnp.int32, (tri_n, tri_n), 1)
    tri = jnp.where(row > col, 1.0, 0.0).astype(BF16)
    keep = (lax.broadcasted_iota(jnp.int32, (tq, tq), 1)
            < lax.broadcasted_iota(jnp.int32, (tq, tq), 0))
    state = {}

    def terms(item, _):
        sid, i, src, k0, w, own, _, _ = item
        q_of, srcs, _ = streams[sid]
        nz = _dot_nt(q_of(i), srcs[src][0](k0, w)) * (-scale2)
        neg_abs = lax.bitcast_convert_type(
            lax.bitcast_convert_type(nz, jnp.uint32) | jnp.uint32(0x80000000), F32)
        lf = jnp.minimum(nz, 0.0) - jnp.log(1.0 + jnp.exp2(neg_abs)) * LOG2E
        if own:
            lf = jnp.where(keep, lf, 0.0)
        hi = lf.astype(BF16)
        lo = (lf - hi.astype(F32)).astype(BF16)
        return nz, lf, (hi, lo), lf.sum(axis=-1, keepdims=True)

    def suffix(item, st):
        w = item[4]
        hi, lo = st[2]
        tri_w = tri[:w, :w]
        if w % LANES == 0:
            within = _dot(jnp.concatenate([hi, lo], axis=1), jnp.concatenate([tri_w, tri_w], axis=0))
        else:
            within = _dot(hi, tri_w) + _dot(lo, tri_w)
        return st + (within,)

    def weigh(item, st):
        sid, i, src, k0, w, own, first, last = item
        _, srcs, put = streams[sid]
        nz, lf, _, total, within = st
        if first:
            state[sid] = (jnp.zeros((tq, 1), F32), jnp.zeros((tq, HEAD_DIM), F32))
        carry, acc = state[sid]
        a = jnp.exp2((lf - nz) + (within + carry))
        if own:
            a = jnp.where(keep, a, 0.0)
        acc = acc + _dot(a.astype(BF16), srcs[src][1](k0, w))
        state[sid] = (carry + total, acc)
        if last:
            put(i, acc)

    items = _sb_items(t // tq, tq, tk, lead)
    _skewed([(sid,) + it for it in items for sid in range(len(streams))], [terms, suffix, weigh])


def _rows_of(ref, c0=0):
    return lambda k0, w: ref[k0:k0 + w, c0:c0 + HEAD_DIM]


def _cached_rows_of(ref, hh):
    return lambda k0, w: ref[k0:k0 + w, hh, :].astype(BF16)


def _sb_prompt_kernel(q_ref, k_ref, v_ref, o_ref, *, tq, **kw):
    def stream(c0):
        def put(i, acc):
            o_ref[i * tq:(i + 1) * tq, c0:c0 + HEAD_DIM] = acc.astype(o_ref.dtype)

        return (lambda i: q_ref[i * tq:(i + 1) * tq, c0:c0 + HEAD_DIM],
                [(_rows_of(k_ref, c0), _rows_of(v_ref, c0))], put)

    _sb_body([stream(c0) for c0 in range(0, o_ref.shape[1], HEAD_DIM)],
             t=q_ref.shape[0], tq=tq, lead=0, **kw)


def _sb_sample_kernel(q_ref, kn_ref, vn_ref, kp_ref, vp_ref, o_in_ref, o_ref, **kw):
    del o_in_ref
    t = q_ref.shape[0]

    def stream(hh):
        c0 = hh * HEAD_DIM

        def put(i, acc):
            o_ref[:, c0:c0 + HEAD_DIM] = acc.astype(o_ref.dtype)

        return (lambda i: q_ref[:, c0:c0 + HEAD_DIM],
                [(_rows_of(kn_ref, c0), _rows_of(vn_ref, c0)),
                 (_cached_rows_of(kp_ref, hh), _cached_rows_of(vp_ref, hh))], put)

    for hh in range(0, kp_ref.shape[1], HEADS_PER_STEP):
        _sb_body([stream(hh + n) for n in range(HEADS_PER_STEP)], t=t, tq=t, lead=kp_ref.shape[0], **kw)


def _sb_prompt_attn(qkv, batch, t, heads, scale):
    tq = _pick(t, (256, 128, 64))
    groups = heads // HEADS_PER_STEP
    spec = lambda g0: pl.BlockSpec((t, HEADS_PER_STEP * HEAD_DIM), lambda b, h: (b, h + g0))
    return pl.pallas_call(
        functools.partial(_sb_prompt_kernel, tq=tq, tk=MXU_DIM, scale2=scale * LOG2E),
        grid=(batch, groups),
        in_specs=[spec(0), spec(groups), spec(2 * groups)],
        out_specs=spec(0),
        out_shape=jax.ShapeDtypeStruct((qkv.shape[0], heads * HEAD_DIM), BF16),
        compiler_params=_params("parallel", "parallel"),
        name="sb_attn_p",
    )(qkv, qkv, qkv)


def _sb_sample_attn(qkv, cache, o_prev, batch, t, heads, row0, scale):
    assert row0 % t == 0
    off = row0 // t
    groups = heads // HEAD_GROUP
    new = lambda g0: pl.BlockSpec((t, HEAD_GROUP * HEAD_DIM), lambda b, hg: (b + off, hg + g0))
    return pl.pallas_call(
        functools.partial(_sb_sample_kernel, tk=MXU_DIM, scale2=scale * LOG2E),
        grid=(batch, groups),
        in_specs=[new(0), new(groups), new(2 * groups), _cache_spec(cache, 0), _cache_spec(cache, groups),
                  pl.BlockSpec(memory_space=pl.ANY)],
        out_specs=new(0),
        out_shape=jax.ShapeDtypeStruct(o_prev.shape, o_prev.dtype),
        input_output_aliases={5: 0},
        compiler_params=_params("parallel", "parallel"),
        name="sb_attn_s",
    )(qkv, qkv, qkv, _cache_view(cache), _cache_view(cache), o_prev)


def _rope_tables(pos):
    half = ROPE_DIM // 2
    inv_freq = ROPE_THETA ** (-jnp.arange(half, dtype=F32) / half)
    ang = pos.astype(F32)[:, None] * inv_freq[None, :]
    cos = jnp.cos(ang)
    sin = jnp.sin(ang)
    zero = jnp.zeros((pos.shape[0], LANES - ROPE_DIM), F32)
    return jnp.concatenate([cos, cos, zero], axis=1), jnp.concatenate([sin, sin, zero], axis=1)


def _rotate_cols(w):
    half = ROPE_DIM // 2
    return jnp.concatenate([-w[..., half:], w[..., :half]], axis=-1)


def _mla_layer(x, g, cosz, sinz, past_ckv, past_krope, w_down, g_q, w_uq, g_kv, w_uk, w_uv, w_o, dims):
    bp, tp, bs, ts = dims
    np_rows = bp * tp
    q_lora = g_q.shape[0]
    kv_lora = g_kv.shape[0]
    heads = w_uk.shape[1]
    past = past_ckv.shape[1]
    scale = (HEAD_DIM + ROPE_DIM) ** -0.5

    w_rope = w_down[:, q_lora + kv_lora:]
    w_down_ext = jnp.concatenate([w_down, _rotate_cols(w_rope)], axis=1).astype(BF16)
    wq = w_uq.reshape(q_lora, heads, HEAD_DIM + ROPE_DIM)
    w_uq_ext = jnp.concatenate([wq, _rotate_cols(wq[..., HEAD_DIM:])], axis=-1)
    w_uq_ext = w_uq_ext.reshape(q_lora, heads * 2 * HEAD_DIM).astype(BF16)
    w_uk2 = w_uk.reshape(kv_lora, heads * HEAD_DIM).astype(BF16)
    w_uv2 = w_uv.reshape(kv_lora, heads * HEAD_DIM).astype(BF16)

    qn, ckv, ckv_b, kr, kr_b = _mla_down(x, g, w_down_ext, g_q, g_kv, cosz, sinz, q_lora, kv_lora)
    q = _mla_q(qn, w_uq_ext, cosz, sinz)
    k_all, v_all = _mla_kv(ckv_b, w_uk2, w_uv2, kr_b)
    kr_past = jnp.pad(past_krope.reshape(bs * past, ROPE_DIM), ((0, 0), (0, LANES - ROPE_DIM)))
    k_past, v_past = _mla_kv(past_ckv.reshape(bs * past, kv_lora), w_uk2, w_uv2, kr_past)

    o = _mla_prompt_attn(q, k_all, v_all, bp, tp, heads, scale)
    o = _mla_sample_attn(q, k_past, v_past, k_all, v_all, o, bs, ts, past, heads, np_rows, scale)
    x = _mm_res(o, w_o.astype(BF16), x)
    outs = (ckv[:np_rows].reshape(bp, tp, kv_lora), kr[:np_rows].reshape(bp, tp, ROPE_DIM),
            ckv[np_rows:].reshape(bs, ts, kv_lora), kr[np_rows:].reshape(bs, ts, ROPE_DIM))
    return x, outs


def _band_bias(rel_bias, nch):
    band = LEFT_CTX + CHUNK
    far = band - 1 - REL_CLIP
    u = jnp.concatenate([jnp.repeat(rel_bias[:, 2 * REL_CLIP:], far, axis=1),
                         rel_bias[:, REL_CLIP - (CHUNK - 1):2 * REL_CLIP + 1][:, ::-1]], axis=1) * LOG2E
    bias = jnp.stack([u[:, CHUNK - 1 - q:CHUNK - 1 - q + band] for q in range(CHUNK)], axis=1).astype(F32)
    rows = [jnp.pad(bias, ((0, 0), (0, 0), (c * CHUNK, (nch - 1 - c) * CHUNK)), constant_values=NEG_INF)
            for c in range(nch)]
    return jnp.concatenate(rows, axis=1)


def _band_layer(x, g, cache, w_qkv, rel_bias, w_o, dims):
    bp, tp, bs, ts = dims
    np_rows = bp * tp
    heads = rel_bias.shape[0]
    hd = heads * HEAD_DIM
    win = cache.shape[1]
    scale = HEAD_DIM ** -0.5

    w = w_qkv.astype(BF16)
    qkv_b, kv_p = _qkv(x, g, w, hd, 0, np_rows)
    qkv_b, kv_s = _qkv(x, g, w, hd, np_rows, bs * ts, qkv_b)
    o = _band_prompt_attn(qkv_b, _band_bias(rel_bias, 2), bp, tp, heads, scale)
    o = _band_sample_attn(qkv_b, cache, _band_bias(rel_bias, 1), o, bs, ts, heads, np_rows, scale)
    x = _mm_res(o, w_o.astype(BF16), x)

    wp = min(LEFT_CTX, tp)
    st_p = kv_p.reshape(bp, tp, 2, heads, HEAD_DIM)[:, tp - wp:]
    new = kv_s.reshape(bs, ts, 2, heads, HEAD_DIM)
    st_s = jnp.concatenate([cache, new], axis=1)[:, ts:]
    return x, (st_p, st_s)


def _sb_layer(x, g, cache, w_qkv, w_o, dims):
    bp, tp, bs, ts = dims
    np_rows = bp * tp
    hd = w_o.shape[0]
    heads = hd // HEAD_DIM
    scale = HEAD_DIM ** -0.5

    w = w_qkv.astype(BF16)
    qkv_b, kv_p = _qkv(x, g, w, hd, 0, np_rows)
    qkv_b, kv_s = _qkv(x, g, w, hd, np_rows, bs * ts, qkv_b)
    o = _sb_prompt_attn(qkv_b, bp, tp, heads, scale)
    o = _sb_sample_attn(qkv_b, cache, o, bs, ts, heads, np_rows, scale)
    x = _mm_res(o, w_o.astype(BF16), x)
    st_p = kv_p.reshape(bp, tp, 2, heads, HEAD_DIM)
    st_s = kv_s.reshape(bs, ts, 2, heads, HEAD_DIM)
    return x, (st_p, st_s)


def kernel(x_prompt, x_sample, cache_mla_ckv, cache_mla_krope, cache_band_kv, cache_sb_kv, norm_g, final_norm_g, ffn_w_gu, ffn_w_down, mla_w_down, mla_q_norm_g, mla_kv_norm_g, mla_w_uq, mla_w_uk, mla_w_uv, mla_w_o, band_w_qkv, band_rel_bias, band_w_o, sb_w_qkv, sb_w_o):
    bp, tp, d = x_prompt.shape
    bs, ts, _ = x_sample.shape
    depth = norm_g.shape[0]
    past = cache_mla_ckv.shape[2]
    dims = (bp, tp, bs, ts)
    np_rows = bp * tp
    ns_rows = bs * ts

    x = jnp.concatenate([x_prompt.reshape(np_rows, d), x_sample.reshape(ns_rows, d)], axis=0)
    w_gu = ffn_w_gu[0, 0].astype(BF16)
    w_dn = ffn_w_down[0, 0].astype(BF16)

    def ffn(x, g, w_gu, w_dn, k):
        if k + 1 == 2 * depth:
            return _ffn(x, g, w_gu, w_dn), None, None
        return _ffn(x, g, w_gu, w_dn, (ffn_w_gu, ffn_w_down, (k + 1) // 2, (k + 1) % 2))

    pos = jnp.concatenate([jnp.tile(jnp.arange(tp), bp), jnp.tile(past + jnp.arange(ts), bs)])
    cosz, sinz = _rope_tables(pos)

    ckv_p, krope_p, band_p, sb_p = [], [], [], []
    ckv_s, krope_s, band_s, sb_s = [], [], [], []
    for i in range(depth):
        kind, j = i % 3, i // 3
        x, w_gu, w_dn = ffn(x, norm_g[i, 0], w_gu, w_dn, 2 * i)
        if kind == 0:
            x, (c_p, k_p, c_s, k_s) = _mla_layer(
                x, norm_g[i, 1], cosz, sinz, cache_mla_ckv[j], cache_mla_krope[j],
                mla_w_down[j], mla_q_norm_g[j], mla_w_uq[j], mla_kv_norm_g[j], mla_w_uk[j], mla_w_uv[j],
                mla_w_o[j], dims)
            ckv_p.append(c_p); krope_p.append(k_p); ckv_s.append(c_s); krope_s.append(k_s)
        elif kind == 1:
            x, (st_p, st_s) = _band_layer(x, norm_g[i, 1], cache_band_kv[j], band_w_qkv[j], band_rel_bias[j],
                                          band_w_o[j], dims)
            band_p.append(st_p); band_s.append(st_s)
        else:
            x, (st_p, st_s) = _sb_layer(x, norm_g[i, 1], cache_sb_kv[j], sb_w_qkv[j], sb_w_o[j], dims)
            sb_p.append(st_p); sb_s.append(st_s)
        x, w_gu, w_dn = ffn(x, norm_g[i, 2], w_gu, w_dn, 2 * i + 1)

    y_prompt = _final_norm(x, final_norm_g, 0, np_rows).reshape(bp, tp, d)
    y_sample = _final_norm(x, final_norm_g, np_rows, ns_rows).reshape(bs, ts, d)
    return (y_prompt, y_sample,
            jnp.stack(ckv_p), jnp.stack(krope_p), jnp.stack(band_p), jnp.stack(sb_p),
            jnp.stack(ckv_s), jnp.stack(krope_s), jnp.stack(band_s), jnp.stack(sb_s))
```

```python
import functools
import math

import jax
import jax.numpy as jnp
from jax import lax
from jax.experimental import pallas as pl
from jax.experimental.pallas import tpu as pltpu

F32 = jnp.float32
BF16 = jnp.bfloat16

EPS = 1e-6
NEG_INF = -1e30
CHUNK = 64
HEAD_DIM = 128
ROPE_DIM = 64
ROPE_THETA = 10000.0
LEFT_CHUNKS = 8
LEFT_CTX = LEFT_CHUNKS * CHUNK
REL_CLIP = 128
LANES = 128
MXU_DIM = 256
HEAD_GROUP = 8
HEADS_PER_STEP = 2
LOG2E = math.log2(math.e)
VMEM_LIMIT = 56 * 1024 * 1024
FFN_VMEM_LIMIT = 60 * 1024 * 1024


def _pick(n, candidates):
    for c in candidates:
        if n % c == 0:
            return c
    raise ValueError(f"no tile in {candidates} divides {n}")


def _params(*sem, vmem=VMEM_LIMIT):
    return pltpu.CompilerParams(dimension_semantics=sem, vmem_limit_bytes=vmem)


def _rms(x, g):
    return x * lax.rsqrt(jnp.mean(x * x, axis=-1, keepdims=True) + EPS) * g


def _dot(a, b):
    return jnp.dot(a, b, preferred_element_type=F32)


def _dot_nt(a, b):
    return lax.dot_general(a, b, (((1,), (1,)), ((), ())), preferred_element_type=F32)


def _skewed(items, stages):
    n, ns = len(items), len(stages)
    live = {}
    for step in range(n + ns - 1):
        for s in range(ns):
            t = step - s
            if 0 <= t < n:
                live[t] = stages[s](items[t], live.get(t))
        live.pop(step - ns + 1, None)


def _cast_slabs(ng_ref, nu_ref, ndn_ref, ngu_out, ndn_out):
    tf = ng_ref.shape[1]
    ngu_out[:, :tf] = ng_ref[...].astype(BF16)
    ngu_out[:, tf:] = nu_ref[...].astype(BF16)
    ndn_out[...] = ndn_ref[...].astype(BF16)


def _cast_kernel(ng_ref, nu_ref, ndn_ref, ngu_out, ndn_out):
    _cast_slabs(ng_ref, nu_ref, ndn_ref, ngu_out, ndn_out)


def _ffn_kernel(*refs, n_col, cast_tiles):
    x_ref, g_ref, wgu_ref, wd_ref = refs[:4]
    if cast_tiles:
        ng_ref, nu_ref, ndn_ref, o_ref, ngu_out, ndn_out, h_ref = refs[4:]
    else:
        o_ref, h_ref = refs[4:]
    tf = wd_ref.shape[0]

    def chunk(first):
        gu = _dot(h_ref[...], wgu_ref[...])
        gate, up = gu[:, :tf], gu[:, tf:]
        a = (gate * jax.nn.sigmoid(gate) * up).astype(BF16)
        base = x_ref if first else o_ref
        for n in range(0, o_ref.shape[1], n_col):
            o_ref[:, n:n + n_col] = base[:, n:n + n_col] + 0.5 * _dot(a, wd_ref[:, n:n + n_col])

    @pl.when(pl.program_id(1) == 0)
    def _():
        h_ref[...] = _rms(x_ref[...], g_ref[...]).astype(BF16)
        chunk(True)

    @pl.when(pl.program_id(1) > 0)
    def _():
        chunk(False)

    if cast_tiles:
        @pl.when(pl.program_id(0) < cast_tiles)
        def _():
            _cast_slabs(ng_ref, nu_ref, ndn_ref, ngu_out, ndn_out)


def _ffn_tf(d_ff):
    return _pick(d_ff, (256, 128))


MAX_CAST_TILES = 8


def _cast_tiles(d, ni):
    return min(MAX_CAST_TILES, ni, d // LANES)


def _cast_specs(d, d_ff, tf, cr, layer, half):
    nf = d_ff // tf
    rows = d // cr
    slab_i = lambda i: jnp.minimum(i, cr - 1)
    slab_j = lambda i, j: jnp.where(i < cr, j, nf - 1)
    in_specs = [
        pl.BlockSpec((None, None, rows, tf), lambda i, j: (layer, half, slab_i(i), slab_j(i, j))),
        pl.BlockSpec((None, None, rows, tf), lambda i, j: (layer, half, slab_i(i), slab_j(i, j) + nf)),
        pl.BlockSpec((None, None, tf, rows), lambda i, j: (layer, half, slab_j(i, j), slab_i(i))),
    ]
    out_specs = [
        pl.BlockSpec((None, rows, 2 * tf), lambda i, j: (slab_j(i, j), slab_i(i), 0)),
        pl.BlockSpec((tf, rows), lambda i, j: (slab_j(i, j), slab_i(i))),
    ]
    out_shape = [jax.ShapeDtypeStruct((nf, d, 2 * tf), BF16), jax.ShapeDtypeStruct((d_ff, d), BF16)]
    return in_specs, out_specs, out_shape


def _cast_ffn_weights(w_gu, w_down, layer, half):
    d, d_ff = w_gu.shape[2], w_down.shape[2]
    tf = _ffn_tf(d_ff)
    cr = _cast_tiles(d, MAX_CAST_TILES)
    in_specs, out_specs, out_shape = _cast_specs(d, d_ff, tf, cr, layer, half)
    return pl.pallas_call(
        _cast_kernel,
        grid=(cr, d_ff // tf),
        in_specs=in_specs,
        out_specs=out_specs,
        out_shape=out_shape,
        compiler_params=_params("parallel", "parallel"),
        name="cast_w",
    )(w_gu, w_gu, w_down)


def _ffn(x, g, w_gu, w_down, nxt=None):
    n, d = x.shape
    d_ff = w_down.shape[0]
    tm = _pick(n, (512, 256, 128))
    tf = _ffn_tf(d_ff)
    ni, nf = n // tm, d_ff // tf
    in_specs = [
        pl.BlockSpec((tm, d), lambda i, j: (i, 0)),
        pl.BlockSpec((1, d), lambda i, j: (0, 0)),
        pl.BlockSpec((None, d, 2 * tf), lambda i, j: (j, 0, 0)),
        pl.BlockSpec((tf, d), lambda i, j: (j, 0)),
    ]
    out_specs = [pl.BlockSpec((tm, d), lambda i, j: (i, 0))]
    out_shape = [jax.ShapeDtypeStruct((n, d), F32)]
    args = [x, g.reshape(1, d), w_gu, w_down]
    cr = 0
    if nxt is not None:
        next_gu, next_dn, layer, half = nxt
        cr = _cast_tiles(d, ni)
        c_in, c_out, c_shape = _cast_specs(d, d_ff, tf, cr, layer, half)
        in_specs += c_in
        out_specs += c_out
        out_shape += c_shape
        args += [next_gu, next_gu, next_dn]
    out = pl.pallas_call(
        functools.partial(_ffn_kernel, n_col=_pick(d, (512, 256, 128)), cast_tiles=cr),
        grid=(ni, nf),
        in_specs=in_specs,
        out_specs=out_specs,
        out_shape=out_shape,
        scratch_shapes=[pltpu.VMEM((tm, d), BF16)],
        compiler_params=_params("arbitrary", "arbitrary", vmem=FFN_VMEM_LIMIT),
        name="ffn",
    )(*args)
    return out if nxt is not None else out[0]


def _qkv_kernel(*refs, aliased):
    x_ref, g_ref, w_ref = refs[:3]
    ob_ref, of_ref, h_ref = refs[4:] if aliased else refs[3:]
    j = pl.program_id(1)

    @pl.when(j == 0)
    def _():
        h_ref[...] = _rms(x_ref[...], g_ref[...]).astype(BF16)

    y = _dot(h_ref[...], w_ref[...])
    ob_ref[...] = y.astype(BF16)
    of_ref[...] = y.reshape(of_ref.shape)


def _qkv(x, g, w, q_cols, row0, rows, qkv_prev=None):
    n, d = x.shape
    nout = w.shape[1]
    tm = _pick(rows, (512, 256, 128))
    tn = _pick(math.gcd(q_cols, nout), (1024, 512, 256, 128))
    qb = q_cols // tn
    assert row0 % tm == 0
    off = row0 // tm
    in_specs = [
        pl.BlockSpec((tm, d), lambda i, j: (i + off, 0)),
        pl.BlockSpec((1, d), lambda i, j: (0, 0)),
        pl.BlockSpec((d, tn), lambda i, j: (0, j)),
    ]
    args = [x, g.reshape(1, d), w]
    if qkv_prev is not None:
        in_specs.append(pl.BlockSpec(memory_space=pl.ANY))
        args.append(qkv_prev)
    return pl.pallas_call(
        functools.partial(_qkv_kernel, aliased=qkv_prev is not None),
        grid=(rows // tm, nout // tn),
        in_specs=in_specs,
        out_specs=[
            pl.BlockSpec((tm, tn), lambda i, j: (i + off, j)),
            pl.BlockSpec((tm, tn // HEAD_DIM, HEAD_DIM), lambda i, j: (i, jnp.maximum(j - qb, 0), 0)),
        ],
        out_shape=[
            jax.ShapeDtypeStruct((n, nout), BF16),
            jax.ShapeDtypeStruct((rows, (nout - q_cols) // HEAD_DIM, HEAD_DIM), F32),
        ],
        input_output_aliases={3: 0} if qkv_prev is not None else {},
        scratch_shapes=[pltpu.VMEM((tm, d), BF16)],
        compiler_params=_params("parallel", "arbitrary"),
        name="qkv",
    )(*args)


def _mm_res_kernel(a_ref, w_ref, r_ref, o_ref):
    o_ref[...] = r_ref[...] + _dot(a_ref[...], w_ref[...])


def _mm_res(a, w, res):
    n, k = a.shape
    nout = w.shape[1]
    tm = _pick(n, (512, 256, 128))
    tn = _pick(nout, (1024, 512, 256, 128))
    return pl.pallas_call(
        _mm_res_kernel,
        grid=(n // tm, nout // tn),
        in_specs=[
            pl.BlockSpec((tm, k), lambda i, j: (i, 0)),
            pl.BlockSpec((k, tn), lambda i, j: (0, j)),
            pl.BlockSpec((tm, tn), lambda i, j: (i, j)),
        ],
        out_specs=pl.BlockSpec((tm, tn), lambda i, j: (i, j)),
        out_shape=jax.ShapeDtypeStruct((n, nout), F32),
        compiler_params=_params("parallel", "parallel"),
        name="mm_res",
    )(a, w, res)


def _final_norm_kernel(x_ref, g_ref, o_ref):
    o_ref[...] = _rms(x_ref[...], g_ref[...])


def _final_norm(x, g, row0, rows):
    d = x.shape[1]
    tm = _pick(rows, (512, 256, 128))
    assert row0 % tm == 0
    off = row0 // tm
    return pl.pallas_call(
        _final_norm_kernel,
        grid=(rows // tm,),
        in_specs=[
            pl.BlockSpec((tm, d), lambda i: (i + off, 0)),
            pl.BlockSpec((1, d), lambda i: (0, 0)),
        ],
        out_specs=pl.BlockSpec((tm, d), lambda i: (i, 0)),
        out_shape=jax.ShapeDtypeStruct((rows, d), F32),
        compiler_params=_params("parallel"),
        name="final_norm",
    )(x, g.reshape(1, d))


def _rope_group(r, cosz, sinz):
    return r * cosz + pltpu.roll(r, ROPE_DIM, axis=1) * sinz


def _mla_down_kernel(x_ref, g_ref, w_ref, gq_ref, gkv_ref, cosz_ref, sinz_ref,
                     qn_ref, ckv_ref, ckvb_ref, kr_ref, krb_ref, *, q_lora, kv_lora):
    h = _rms(x_ref[...], g_ref[...]).astype(BF16)
    y = _dot(h, w_ref[...])
    qn_ref[...] = _rms(y[:, :q_lora], gq_ref[...]).astype(BF16)
    ckv = _rms(y[:, q_lora:q_lora + kv_lora], gkv_ref[...])
    ckv_ref[...] = ckv
    ckvb_ref[...] = ckv.astype(BF16)
    r0 = q_lora + kv_lora
    kr = _rope_group(y[:, r0:r0 + 2 * ROPE_DIM], cosz_ref[...], sinz_ref[...])
    kr_ref[...] = kr[:, :ROPE_DIM]
    krb_ref[...] = kr.astype(BF16)


def _mla_down(x, g, w_ext, gq, gkv, cosz, sinz, q_lora, kv_lora):
    n, d = x.shape
    wn = w_ext.shape[1]
    tm = _pick(n, (256, 128))
    row = lambda i: (i, 0)
    fix = lambda i: (0, 0)
    return pl.pallas_call(
        functools.partial(_mla_down_kernel, q_lora=q_lora, kv_lora=kv_lora),
        grid=(n // tm,),
        in_specs=[
            pl.BlockSpec((tm, d), row),
            pl.BlockSpec((1, d), fix),
            pl.BlockSpec((d, wn), fix),
            pl.BlockSpec((1, q_lora), fix),
            pl.BlockSpec((1, kv_lora), fix),
            pl.BlockSpec((tm, LANES), row),
            pl.BlockSpec((tm, LANES), row),
        ],
        out_specs=[
            pl.BlockSpec((tm, q_lora), row),
            pl.BlockSpec((tm, kv_lora), row),
            pl.BlockSpec((tm, kv_lora), row),
            pl.BlockSpec((tm, ROPE_DIM), row),
            pl.BlockSpec((tm, 2 * ROPE_DIM), row),
        ],
        out_shape=[
            jax.ShapeDtypeStruct((n, q_lora), BF16),
            jax.ShapeDtypeStruct((n, kv_lora), F32),
            jax.ShapeDtypeStruct((n, kv_lora), BF16),
            jax.ShapeDtypeStruct((n, ROPE_DIM), F32),
            jax.ShapeDtypeStruct((n, 2 * ROPE_DIM), BF16),
        ],
        compiler_params=_params("parallel"),
        name="mla_down",
    )(x, g.reshape(1, d), w_ext, gq.reshape(1, -1), gkv.reshape(1, -1), cosz, sinz)


def _mla_q_kernel(a_ref, w_ref, cosz_ref, sinz_ref, o_ref):
    y = _dot(a_ref[...], w_ref[...])
    cosz = cosz_ref[...]
    sinz = sinz_ref[...]
    for c in range(0, y.shape[1], 2 * HEAD_DIM):
        rot = _rope_group(y[:, c + HEAD_DIM:c + 2 * HEAD_DIM], cosz, sinz)
        o_ref[:, c:c + HEAD_DIM] = y[:, c:c + HEAD_DIM].astype(BF16)
        o_ref[:, c + HEAD_DIM:c + 2 * HEAD_DIM] = rot.astype(BF16)


def _mla_q(qn, w_ext, cosz, sinz):
    n, k = qn.shape
    nout = w_ext.shape[1]
    tm = _pick(n, (512, 256, 128))
    tn = _pick(nout, (2048, 1024, 512, 256))
    return pl.pallas_call(
        _mla_q_kernel,
        grid=(n // tm, nout // tn),
        in_specs=[
            pl.BlockSpec((tm, k), lambda i, j: (i, 0)),
            pl.BlockSpec((k, tn), lambda i, j: (0, j)),
            pl.BlockSpec((tm, LANES), lambda i, j: (i, 0)),
            pl.BlockSpec((tm, LANES), lambda i, j: (i, 0)),
        ],
        out_specs=pl.BlockSpec((tm, tn), lambda i, j: (i, j)),
        out_shape=jax.ShapeDtypeStruct((n, nout), BF16),
        compiler_params=_params("parallel", "parallel"),
        name="mla_q",
    )(qn, w_ext, cosz, sinz)


def _mla_kv_kernel(c_ref, wk_ref, wv_ref, kr_ref, k_ref, v_ref):
    c = c_ref[...].astype(BF16)
    yk = _dot(c, wk_ref[...]).astype(BF16)
    v_ref[...] = _dot(c, wv_ref[...]).astype(BF16)
    kr = kr_ref[...].astype(BF16)
    for hh in range(yk.shape[1] // HEAD_DIM):
        k_ref[:, 2 * hh * HEAD_DIM:(2 * hh + 1) * HEAD_DIM] = yk[:, hh * HEAD_DIM:(hh + 1) * HEAD_DIM]
        k_ref[:, (2 * hh + 1) * HEAD_DIM:(2 * hh + 2) * HEAD_DIM] = kr


def _mla_kv(c, w_uk, w_uv, kr):
    n, k = c.shape
    hd = w_uk.shape[1]
    tm = _pick(n, (512, 256, 128))
    tn = _pick(hd, (2048, 1024, 512, 256, 128))
    return pl.pallas_call(
        _mla_kv_kernel,
        grid=(n // tm, hd // tn),
        in_specs=[
            pl.BlockSpec((tm, k), lambda i, j: (i, 0)),
            pl.BlockSpec((k, tn), lambda i, j: (0, j)),
            pl.BlockSpec((k, tn), lambda i, j: (0, j)),
            pl.BlockSpec((tm, LANES), lambda i, j: (i, 0)),
        ],
        out_specs=[
            pl.BlockSpec((tm, 2 * tn), lambda i, j: (i, j)),
            pl.BlockSpec((tm, tn), lambda i, j: (i, j)),
        ],
        out_shape=[
            jax.ShapeDtypeStruct((n, 2 * hd), BF16),
            jax.ShapeDtypeStruct((n, hd), BF16),
        ],
        compiler_params=_params("parallel", "parallel"),
        name="mla_kv",
    )(c, w_uk, w_uv, kr)


def _softmax_pv(parts):
    m = parts[0][0].max(axis=-1, keepdims=True)
    for t, _ in parts[1:]:
        m = jnp.maximum(m, t.max(axis=-1, keepdims=True))
    l = None
    acc = None
    for t, v in parts:
        p = jnp.exp2(t - m)
        ps = p.sum(axis=-1, keepdims=True)
        pv = _dot(p.astype(BF16), v)
        l = ps if l is None else l + ps
        acc = pv if acc is None else acc + pv
    return acc / l


def _interleave(n_streams, n_items):
    return [(hh, i) for i in range(n_items) for hh in range(n_streams)]


def _mla_prompt_kernel(q_ref, k_ref, v_ref, o_ref, *, tq, scale2):
    t = q_ref.shape[0]
    r = lax.broadcasted_iota(jnp.int32, (tq, tq), 0)
    c = lax.broadcasted_iota(jnp.int32, (tq, tq), 1)
    diag = c < (r | (CHUNK - 1)) + 1

    def scores(item, _):
        hh, i = item
        qk = slice(hh * 2 * HEAD_DIM, (hh + 1) * 2 * HEAD_DIM)
        return _dot_nt(q_ref[i * tq:(i + 1) * tq, qk], k_ref[0:(i + 1) * tq, qk]) * scale2

    def finish(item, s):
        hh, i = item
        vo = slice(hh * HEAD_DIM, (hh + 1) * HEAD_DIM)
        lo = i * tq
        parts = [(s[:, :lo], v_ref[0:lo, vo])] if i else []
        parts.append((jnp.where(diag, s[:, lo:], NEG_INF), v_ref[lo:lo + tq, vo]))
        o_ref[lo:lo + tq, vo] = _softmax_pv(parts).astype(o_ref.dtype)

    _skewed(_interleave(o_ref.shape[1] // HEAD_DIM, t // tq), [scores, finish])


def _mla_prompt_attn(q, k, v, batch, t, heads, scale):
    tq = _pick(t, (256, 128))
    hps = HEADS_PER_STEP
    return pl.pallas_call(
        functools.partial(_mla_prompt_kernel, tq=tq, scale2=scale * LOG2E),
        grid=(batch, heads // hps),
        in_specs=[
            pl.BlockSpec((t, hps * 2 * HEAD_DIM), lambda b, h: (b, h)),
            pl.BlockSpec((t, hps * 2 * HEAD_DIM), lambda b, h: (b, h)),
            pl.BlockSpec((t, hps * HEAD_DIM), lambda b, h: (b, h)),
        ],
        out_specs=pl.BlockSpec((t, hps * HEAD_DIM), lambda b, h: (b, h)),
        out_shape=jax.ShapeDtypeStruct((q.shape[0], heads * HEAD_DIM), BF16),
        compiler_params=_params("parallel", "parallel"),
        name="mla_attn_p",
    )(q, k, v)


def _mla_sample_kernel(q_ref, kp_ref, vp_ref, kn_ref, vn_ref, o_in_ref, o_ref, *, scale2):
    del o_in_ref
    for hh in range(o_ref.shape[1] // HEAD_DIM):
        qk = slice(hh * 2 * HEAD_DIM, (hh + 1) * 2 * HEAD_DIM)
        vo = slice(hh * HEAD_DIM, (hh + 1) * HEAD_DIM)
        q = q_ref[:, qk]
        parts = [(_dot_nt(q, kp_ref[:, qk]) * scale2, vp_ref[:, vo]),
                 (_dot_nt(q, kn_ref[:, qk]) * scale2, vn_ref[:, vo])]
        o_ref[:, vo] = _softmax_pv(parts).astype(o_ref.dtype)


def _mla_sample_attn(q, k_past, v_past, k_all, v_all, o_prev, batch, t, past, heads, row0, scale):
    assert row0 % t == 0
    off = row0 // t
    hps = _pick(heads, (4, 2, 1))
    new = lambda width: pl.BlockSpec((t, hps * width), lambda b, h: (b + off, h))
    old = lambda width: pl.BlockSpec((past, hps * width), lambda b, h: (b, h))
    return pl.pallas_call(
        functools.partial(_mla_sample_kernel, scale2=scale * LOG2E),
        grid=(batch, heads // hps),
        in_specs=[new(2 * HEAD_DIM), old(2 * HEAD_DIM), old(HEAD_DIM), new(2 * HEAD_DIM), new(HEAD_DIM),
                  pl.BlockSpec(memory_space=pl.ANY)],
        out_specs=new(HEAD_DIM),
        out_shape=jax.ShapeDtypeStruct(o_prev.shape, o_prev.dtype),
        input_output_aliases={5: 0},
        compiler_params=_params("parallel", "parallel"),
        name="mla_attn_s",
    )(q, k_past, v_past, k_all, v_all, o_prev)


def _band_prompt_kernel(q_ref, k_ref, v_ref, bias_ref, o_ref, *, tq, scale2):
    t = q_ref.shape[0]
    band = bias_ref.shape[2]

    def scores(item, _):
        hh, i = item
        cols = slice(hh * HEAD_DIM, (hh + 1) * HEAD_DIM)
        ke = (i + 1) * tq
        ks = max(0, ke - band)
        s = _dot_nt(q_ref[i * tq:(i + 1) * tq, cols], k_ref[ks:ke, cols])
        return s * scale2 + bias_ref[hh, :, band - (ke - ks):]

    def finish(item, s):
        hh, i = item
        cols = slice(hh * HEAD_DIM, (hh + 1) * HEAD_DIM)
        ke = (i + 1) * tq
        ks = max(0, ke - band)
        o_ref[i * tq:(i + 1) * tq, cols] = _softmax_pv([(s, v_ref[ks:ke, cols])]).astype(o_ref.dtype)

    _skewed(_interleave(bias_ref.shape[0], t // tq), [scores, finish])


def _band_prompt_attn(qkv, bias2, batch, t, heads, scale):
    hps = HEADS_PER_STEP
    groups = heads // hps
    spec = lambda g0: pl.BlockSpec((t, hps * HEAD_DIM), lambda b, h: (b, h + g0))
    return pl.pallas_call(
        functools.partial(_band_prompt_kernel, tq=bias2.shape[1], scale2=scale * LOG2E),
        grid=(batch, groups),
        in_specs=[spec(0), spec(groups), spec(2 * groups),
                  pl.BlockSpec((hps,) + bias2.shape[1:], lambda b, h: (h, 0, 0))],
        out_specs=spec(0),
        out_shape=jax.ShapeDtypeStruct((qkv.shape[0], heads * HEAD_DIM), BF16),
        compiler_params=_params("parallel", "parallel"),
        name="band_attn_p",
    )(qkv, qkv, qkv, bias2)


def _band_sample_kernel(q_ref, kp_ref, vp_ref, kn_ref, vn_ref, bias_ref, o_in_ref, o_ref, *, scale2):
    del o_in_ref
    win = kp_ref.shape[0]
    for hh in range(kp_ref.shape[1]):
        cols = slice(hh * HEAD_DIM, (hh + 1) * HEAD_DIM)
        q = q_ref[:, cols]
        parts = [(_dot_nt(q, kp_ref[:, hh, :].astype(BF16)) * scale2 + bias_ref[hh, :, :win],
                  vp_ref[:, hh, :].astype(BF16)),
                 (_dot_nt(q, kn_ref[:, cols]) * scale2 + bias_ref[hh, :, win:], vn_ref[:, cols])]
        o_ref[:, cols] = _softmax_pv(parts).astype(o_ref.dtype)


def _cache_view(cache):
    return cache.reshape(cache.shape[:2] + (-1, HEAD_DIM))


def _cache_spec(cache, g0):
    return pl.BlockSpec((None, cache.shape[1], HEAD_GROUP, HEAD_DIM), lambda b, hg: (b, 0, hg + g0, 0))


def _band_sample_attn(qkv, cache, bias, o_prev, batch, t, heads, row0, scale):
    assert row0 % t == 0
    off = row0 // t
    groups = heads // HEAD_GROUP
    new = lambda g0: pl.BlockSpec((t, HEAD_GROUP * HEAD_DIM), lambda b, hg: (b + off, hg + g0))
    return pl.pallas_call(
        functools.partial(_band_sample_kernel, scale2=scale * LOG2E),
        grid=(batch, groups),
        in_specs=[new(0), _cache_spec(cache, 0), _cache_spec(cache, groups), new(groups), new(2 * groups),
                  pl.BlockSpec((HEAD_GROUP,) + bias.shape[1:], lambda b, hg: (hg, 0, 0)),
                  pl.BlockSpec(memory_space=pl.ANY)],
        out_specs=new(0),
        out_shape=jax.ShapeDtypeStruct(o_prev.shape, o_prev.dtype),
        input_output_aliases={6: 0},
        compiler_params=_params("parallel", "parallel"),
        name="band_attn_s",
    )(qkv, _cache_view(cache), _cache_view(cache), qkv, qkv, bias, o_prev)


def _sb_items(n_q, tq, tk, lead):
    items = []
    for i in range(n_q):
        blocks = [(0, i * tq, tq, True)]
        end = i * tq
        while end > 0:
            w = min(tk, end)
            blocks.append((0, end - w, w, False))
            end -= w
        end = lead
        while end > 0:
            w = min(tk, end)
            blocks.append((1, end - w, w, False))
            end -= w
        for n, (src, k0, w, own) in enumerate(blocks):
            items.append((i, src, k0, w, own, n == 0, n == len(blocks) - 1))
    return items


def _sb_body(streams, *, t, tq, tk, lead, scale2):
    tri_n = max(tq, tk)
    row = lax.broadcasted_iota(jnp.int32, (tri_n, tri_n), 0)
    col = lax.broadcasted_iota(jnp.int32, (tri_n, tri_n), 1)
    tri = jnp.where(row > col, 1.0, 0.0).astype(BF16)
    keep = (lax.broadcasted_iota(jnp.int32, (tq, tq), 1)
            < lax.broadcasted_iota(jnp.int32, (tq, tq), 0))
    state = {}

    def terms(item, _):
        sid, i, src, k0, w, own, _, _ = item
        q_of, srcs, _ = streams[sid]
        nz = _dot_nt(q_of(i), srcs[src][0](k0, w)) * (-scale2)
        neg_abs = lax.bitcast_convert_type(
            lax.bitcast_convert_type(nz, jnp.uint32) | jnp.uint32(0x80000000), F32)
        lf = jnp.minimum(nz, 0.0) - jnp.log(1.0 + jnp.exp2(neg_abs)) * LOG2E
        if own:
            lf = jnp.where(keep, lf, 0.0)
        hi = lf.astype(BF16)
        lo = (lf - hi.astype(F32)).astype(BF16)
        return nz, lf, (hi, lo), lf.sum(axis=-1, keepdims=True)

    def suffix(item, st):
        w = item[4]
        hi, lo = st[2]
        tri_w = tri[:w, :w]
        if w % LANES == 0:
            within = _dot(jnp.concatenate([hi, lo], axis=1), jnp.concatenate([tri_w, tri_w], axis=0))
        else:
            within = _dot(hi, tri_w) + _dot(lo, tri_w)
        return st + (within,)

    def weigh(item, st):
        sid, i, src, k0, w, own, first, last = item
        _, srcs, put = streams[sid]
        nz, lf, _, total, within = st
        if first:
            state[sid] = (jnp.zeros((tq, 1), F32), jnp.zeros((tq, HEAD_DIM), F32))
        carry, acc = state[sid]
        a = jnp.exp2((lf - nz) + (within + carry))
        if own:
            a = jnp.where(keep, a, 0.0)
        acc = acc + _dot(a.astype(BF16), srcs[src][1](k0, w))
        state[sid] = (carry + total, acc)
        if last:
            put(i, acc)

    items = _sb_items(t // tq, tq, tk, lead)
    _skewed([(sid,) + it for it in items for sid in range(len(streams))], [terms, suffix, weigh])


def _rows_of(ref, c0=0):
    return lambda k0, w: ref[k0:k0 + w, c0:c0 + HEAD_DIM]


def _cached_rows_of(ref, hh):
    return lambda k0, w: ref[k0:k0 + w, hh, :].astype(BF16)


def _sb_prompt_kernel(q_ref, k_ref, v_ref, o_ref, *, tq, **kw):
    def stream(c0):
        def put(i, acc):
            o_ref[i * tq:(i + 1) * tq, c0:c0 + HEAD_DIM] = acc.astype(o_ref.dtype)

        return (lambda i: q_ref[i * tq:(i + 1) * tq, c0:c0 + HEAD_DIM],
                [(_rows_of(k_ref, c0), _rows_of(v_ref, c0))], put)

    _sb_body([stream(c0) for c0 in range(0, o_ref.shape[1], HEAD_DIM)],
             t=q_ref.shape[0], tq=tq, lead=0, **kw)


def _sb_sample_kernel(q_ref, kn_ref, vn_ref, kp_ref, vp_ref, o_in_ref, o_ref, **kw):
    del o_in_ref
    t = q_ref.shape[0]

    def stream(hh):
        c0 = hh * HEAD_DIM

        def put(i, acc):
            o_ref[:, c0:c0 + HEAD_DIM] = acc.astype(o_ref.dtype)

        return (lambda i: q_ref[:, c0:c0 + HEAD_DIM],
                [(_rows_of(kn_ref, c0), _rows_of(vn_ref, c0)),
                 (_cached_rows_of(kp_ref, hh), _cached_rows_of(vp_ref, hh))], put)

    for hh in range(0, kp_ref.shape[1], HEADS_PER_STEP):
        _sb_body([stream(hh + n) for n in range(HEADS_PER_STEP)], t=t, tq=t, lead=kp_ref.shape[0], **kw)


def _sb_prompt_attn(qkv, batch, t, heads, scale):
    tq = _pick(t, (256, 128, 64))
    groups = heads // HEADS_PER_STEP
    spec = lambda g0: pl.BlockSpec((t, HEADS_PER_STEP * HEAD_DIM), lambda b, h: (b, h + g0))
    return pl.pallas_call(
        functools.partial(_sb_prompt_kernel, tq=tq, tk=MXU_DIM, scale2=scale * LOG2E),
        grid=(batch, groups),
        in_specs=[spec(0), spec(groups), spec(2 * groups)],
        out_specs=spec(0),
        out_shape=jax.ShapeDtypeStruct((qkv.shape[0], heads * HEAD_DIM), BF16),
        compiler_params=_params("parallel", "parallel"),
        name="sb_attn_p",
    )(qkv, qkv, qkv)


def _sb_sample_attn(qkv, cache, o_prev, batch, t, heads, row0, scale):
    assert row0 % t == 0
    off = row0 // t
    groups = heads // HEAD_GROUP
    new = lambda g0: pl.BlockSpec((t, HEAD_GROUP * HEAD_DIM), lambda b, hg: (b + off, hg + g0))
    return pl.pallas_call(
        functools.partial(_sb_sample_kernel, tk=MXU_DIM, scale2=scale * LOG2E),
        grid=(batch, groups),
        in_specs=[new(0), new(groups), new(2 * groups), _cache_spec(cache, 0), _cache_spec(cache, groups),
                  pl.BlockSpec(memory_space=pl.ANY)],
        out_specs=new(0),
        out_shape=jax.ShapeDtypeStruct(o_prev.shape, o_prev.dtype),
        input_output_aliases={5: 0},
        compiler_params=_params("parallel", "parallel"),
        name="sb_attn_s",
    )(qkv, qkv, qkv, _cache_view(cache), _cache_view(cache), o_prev)


def _rope_tables(pos):
    half = ROPE_DIM // 2
    inv_freq = ROPE_THETA ** (-jnp.arange(half, dtype=F32) / half)
    ang = pos.astype(F32)[:, None] * inv_freq[None, :]
    cos = jnp.cos(ang)
    sin = jnp.sin(ang)
    zero = jnp.zeros((pos.shape[0], LANES - ROPE_DIM), F32)
    return jnp.concatenate([cos, cos, zero], axis=1), jnp.concatenate([sin, sin, zero], axis=1)


def _rotate_cols(w):
    half = ROPE_DIM // 2
    return jnp.concatenate([-w[..., half:], w[..., :half]], axis=-1)


def _mla_layer(x, g, cosz, sinz, past_ckv, past_krope, w_down, g_q, w_uq, g_kv, w_uk, w_uv, w_o, dims):
    bp, tp, bs, ts = dims
    np_rows = bp * tp
    q_lora = g_q.shape[0]
    kv_lora = g_kv.shape[0]
    heads = w_uk.shape[1]
    past = past_ckv.shape[1]
    scale = (HEAD_DIM + ROPE_DIM) ** -0.5

    w_rope = w_down[:, q_lora + kv_lora:]
    w_down_ext = jnp.concatenate([w_down, _rotate_cols(w_rope)], axis=1).astype(BF16)
    wq = w_uq.reshape(q_lora, heads, HEAD_DIM + ROPE_DIM)
    w_uq_ext = jnp.concatenate([wq, _rotate_cols(wq[..., HEAD_DIM:])], axis=-1)
    w_uq_ext = w_uq_ext.reshape(q_lora, heads * 2 * HEAD_DIM).astype(BF16)
    w_uk2 = w_uk.reshape(kv_lora, heads * HEAD_DIM).astype(BF16)
    w_uv2 = w_uv.reshape(kv_lora, heads * HEAD_DIM).astype(BF16)

    qn, ckv, ckv_b, kr, kr_b = _mla_down(x, g, w_down_ext, g_q, g_kv, cosz, sinz, q_lora, kv_lora)
    q = _mla_q(qn, w_uq_ext, cosz, sinz)
    k_all, v_all = _mla_kv(ckv_b, w_uk2, w_uv2, kr_b)
    kr_past = jnp.pad(past_krope.reshape(bs * past, ROPE_DIM), ((0, 0), (0, LANES - ROPE_DIM)))
    k_past, v_past = _mla_kv(past_ckv.reshape(bs * past, kv_lora), w_uk2, w_uv2, kr_past)

    o = _mla_prompt_attn(q, k_all, v_all, bp, tp, heads, scale)
    o = _mla_sample_attn(q, k_past, v_past, k_all, v_all, o, bs, ts, past, heads, np_rows, scale)
    x = _mm_res(o, w_o.astype(BF16), x)
    outs = (ckv[:np_rows].reshape(bp, tp, kv_lora), kr[:np_rows].reshape(bp, tp, ROPE_DIM),
            ckv[np_rows:].reshape(bs, ts, kv_lora), kr[np_rows:].reshape(bs, ts, ROPE_DIM))
    return x, outs


def _band_bias(rel_bias, nch):
    band = LEFT_CTX + CHUNK
    far = band - 1 - REL_CLIP
    u = jnp.concatenate([jnp.repeat(rel_bias[:, 2 * REL_CLIP:], far, axis=1),
                         rel_bias[:, REL_CLIP - (CHUNK - 1):2 * REL_CLIP + 1][:, ::-1]], axis=1) * LOG2E
    bias = jnp.stack([u[:, CHUNK - 1 - q:CHUNK - 1 - q + band] for q in range(CHUNK)], axis=1).astype(F32)
    rows = [jnp.pad(bias, ((0, 0), (0, 0), (c * CHUNK, (nch - 1 - c) * CHUNK)), constant_values=NEG_INF)
            for c in range(nch)]
    return jnp.concatenate(rows, axis=1)


def _band_layer(x, g, cache, w_qkv, rel_bias, w_o, dims):
    bp, tp, bs, ts = dims
    np_rows = bp * tp
    heads = rel_bias.shape[0]
    hd = heads * HEAD_DIM
    win = cache.shape[1]
    scale = HEAD_DIM ** -0.5

    w = w_qkv.astype(BF16)
    qkv_b, kv_p = _qkv(x, g, w, hd, 0, np_rows)
    qkv_b, kv_s = _qkv(x, g, w, hd, np_rows, bs * ts, qkv_b)
    o = _band_prompt_attn(qkv_b, _band_bias(rel_bias, 2), bp, tp, heads, scale)
    o = _band_sample_attn(qkv_b, cache, _band_bias(rel_bias, 1), o, bs, ts, heads, np_rows, scale)
    x = _mm_res(o, w_o.astype(BF16), x)

    wp = min(LEFT_CTX, tp)
    st_p = kv_p.reshape(bp, tp, 2, heads, HEAD_DIM)[:, tp - wp:]
    new = kv_s.reshape(bs, ts, 2, heads, HEAD_DIM)
    st_s = jnp.concatenate([cache, new], axis=1)[:, ts:]
    return x, (st_p, st_s)


def _sb_layer(x, g, cache, w_qkv, w_o, dims):
    bp, tp, bs, ts = dims
    np_rows = bp * tp
    hd = w_o.shape[0]
    heads = hd // HEAD_DIM
    scale = HEAD_DIM ** -0.5

    w = w_qkv.astype(BF16)
    qkv_b, kv_p = _qkv(x, g, w, hd, 0, np_rows)
    qkv_b, kv_s = _qkv(x, g, w, hd, np_rows, bs * ts, qkv_b)
    o = _sb_prompt_attn(qkv_b, bp, tp, heads, scale)
    o = _sb_sample_attn(qkv_b, cache, o, bs, ts, heads, np_rows, scale)
    x = _mm_res(o, w_o.astype(BF16), x)
    st_p = kv_p.reshape(bp, tp, 2, heads, HEAD_DIM)
    st_s = kv_s.reshape(bs, ts, 2, heads, HEAD_DIM)
    return x, (st_p, st_s)


def kernel(x_prompt, x_sample, cache_mla_ckv, cache_mla_krope, cache_band_kv, cache_sb_kv, norm_g, final_norm_g, ffn_w_gu, ffn_w_down, mla_w_down, mla_q_norm_g, mla_kv_norm_g, mla_w_uq, mla_w_uk, mla_w_uv, mla_w_o, band_w_qkv, band_rel_bias, band_w_o, sb_w_qkv, sb_w_o):
    bp, tp, d = x_prompt.shape
    bs, ts, _ = x_sample.shape
    depth = norm_g.shape[0]
    past = cache_mla_ckv.shape[2]
    dims = (bp, tp, bs, ts)
    np_rows = bp * tp
    ns_rows = bs * ts

    x = jnp.concatenate([x_prompt.reshape(np_rows, d), x_sample.reshape(ns_rows, d)], axis=0)
    w_gu, w_dn = _cast_ffn_weights(ffn_w_gu, ffn_w_down, 0, 0)

    def ffn(x, g, w_gu, w_dn, k):
        if k + 1 == 2 * depth:
            return _ffn(x, g, w_gu, w_dn), None, None
        return _ffn(x, g, w_gu, w_dn, (ffn_w_gu, ffn_w_down, (k + 1) // 2, (k + 1) % 2))

    pos = jnp.concatenate([jnp.tile(jnp.arange(tp), bp), jnp.tile(past + jnp.arange(ts), bs)])
    cosz, sinz = _rope_tables(pos)

    ckv_p, krope_p, band_p, sb_p = [], [], [], []
    ckv_s, krope_s, band_s, sb_s = [], [], [], []
    for i in range(depth):
        kind, j = i % 3, i // 3
        x, w_gu, w_dn = ffn(x, norm_g[i, 0], w_gu, w_dn, 2 * i)
        if kind == 0:
            x, (c_p, k_p, c_s, k_s) = _mla_layer(
                x, norm_g[i, 1], cosz, sinz, cache_mla_ckv[j], cache_mla_krope[j],
                mla_w_down[j], mla_q_norm_g[j], mla_w_uq[j], mla_kv_norm_g[j], mla_w_uk[j], mla_w_uv[j],
                mla_w_o[j], dims)
            ckv_p.append(c_p); krope_p.append(k_p); ckv_s.append(c_s); krope_s.append(k_s)
        elif kind == 1:
            x, (st_p, st_s) = _band_layer(x, norm_g[i, 1], cache_band_kv[j], band_w_qkv[j], band_rel_bias[j],
                                          band_w_o[j], dims)
            band_p.append(st_p); band_s.append(st_s)
        else:
            x, (st_p, st_s) = _sb_layer(x, norm_g[i, 1], cache_sb_kv[j], sb_w_qkv[j], sb_w_o[j], dims)
            sb_p.append(st_p); sb_s.append(st_s)
        x, w_gu, w_dn = ffn(x, norm_g[i, 2], w_gu, w_dn, 2 * i + 1)

    y_prompt = _final_norm(x, final_norm_g, 0, np_rows).reshape(bp, tp, d)
    y_sample = _final_norm(x, final_norm_g, np_rows, ns_rows).reshape(bs, ts, d)
    return (y_prompt, y_sample,
            jnp.stack(ckv_p), jnp.stack(krope_p), jnp.stack(band_p), jnp.stack(sb_p),
            jnp.stack(ckv_s), jnp.stack(krope_s), jnp.stack(band_s), jnp.stack(sb_s))
```

```python
import functools
import math

import jax
import jax.numpy as jnp
from jax import lax
from jax.experimental import pallas as pl
from jax.experimental.pallas import tpu as pltpu

F32 = jnp.float32
BF16 = jnp.bfloat16

EPS = 1e-6
NEG_INF = -1e30
CHUNK = 64
HEAD_DIM = 128
ROPE_DIM = 64
ROPE_THETA = 10000.0
LEFT_CHUNKS = 8
LEFT_CTX = LEFT_CHUNKS * CHUNK
REL_CLIP = 128
LANES = 128
MXU_DIM = 256
HEAD_GROUP = 8
HEADS_PER_STEP = 2
LOG2E = math.log2(math.e)
VMEM_LIMIT = 56 * 1024 * 1024
FFN_VMEM_LIMIT = 60 * 1024 * 1024


def _pick(n, candidates):
    for c in candidates:
        if n % c == 0:
            return c
    raise ValueError(f"no tile in {candidates} divides {n}")


def _params(*sem, vmem=VMEM_LIMIT):
    return pltpu.CompilerParams(dimension_semantics=sem, vmem_limit_bytes=vmem)


def _rms(x, g):
    return x * lax.rsqrt(jnp.mean(x * x, axis=-1, keepdims=True) + EPS) * g


def _dot(a, b):
    return jnp.dot(a, b, preferred_element_type=F32)


def _dot_nt(a, b):
    return lax.dot_general(a, b, (((1,), (1,)), ((), ())), preferred_element_type=F32)


def _skewed(items, stages):
    n, ns = len(items), len(stages)
    live = {}
    for step in range(n + ns - 1):
        for s in range(ns):
            t = step - s
            if 0 <= t < n:
                live[t] = stages[s](items[t], live.get(t))
        live.pop(step - ns + 1, None)


def _cast_slabs(ng_ref, nu_ref, ndn_ref, ngu_out, ndn_out):
    tf = ng_ref.shape[1]
    ngu_out[:, :tf] = ng_ref[...].astype(BF16)
    ngu_out[:, tf:] = nu_ref[...].astype(BF16)
    ndn_out[...] = ndn_ref[...].astype(BF16)


def _cast_kernel(ng_ref, nu_ref, ndn_ref, ngu_out, ndn_out):
    _cast_slabs(ng_ref, nu_ref, ndn_ref, ngu_out, ndn_out)


def _ffn_kernel(*refs, n_col, cast_tiles):
    x_ref, g_ref, wgu_ref, wd_ref = refs[:4]
    if cast_tiles:
        ng_ref, nu_ref, ndn_ref, o_ref, ngu_out, ndn_out, h_ref = refs[4:]
    else:
        o_ref, h_ref = refs[4:]
    tf = wd_ref.shape[0]

    def chunk(first):
        gu = _dot(h_ref[...], wgu_ref[...])
        gate, up = gu[:, :tf], gu[:, tf:]
        a = (gate * jax.nn.sigmoid(gate) * up).astype(BF16)
        base = x_ref if first else o_ref
        for n in range(0, o_ref.shape[1], n_col):
            o_ref[:, n:n + n_col] = base[:, n:n + n_col] + 0.5 * _dot(a, wd_ref[:, n:n + n_col])

    @pl.when(pl.program_id(1) == 0)
    def _():
        h_ref[...] = _rms(x_ref[...], g_ref[...]).astype(BF16)
        chunk(True)

    @pl.when(pl.program_id(1) > 0)
    def _():
        chunk(False)

    if cast_tiles:
        @pl.when(pl.program_id(0) < cast_tiles)
        def _():
            _cast_slabs(ng_ref, nu_ref, ndn_ref, ngu_out, ndn_out)


def _ffn_tf(d_ff):
    return _pick(d_ff, (256, 128))


MAX_CAST_TILES = 8


def _cast_tiles(d, ni):
    return min(MAX_CAST_TILES, ni, d // LANES)


def _cast_specs(d, d_ff, tf, cr, layer, half):
    nf = d_ff // tf
    rows = d // cr
    slab_i = lambda i: jnp.minimum(i, cr - 1)
    slab_j = lambda i, j: jnp.where(i < cr, j, nf - 1)
    in_specs = [
        pl.BlockSpec((None, None, rows, tf), lambda i, j: (layer, half, slab_i(i), slab_j(i, j))),
        pl.BlockSpec((None, None, rows, tf), lambda i, j: (layer, half, slab_i(i), slab_j(i, j) + nf)),
        pl.BlockSpec((None, None, tf, rows), lambda i, j: (layer, half, slab_j(i, j), slab_i(i))),
    ]
    out_specs = [
        pl.BlockSpec((None, rows, 2 * tf), lambda i, j: (slab_j(i, j), slab_i(i), 0)),
        pl.BlockSpec((tf, rows), lambda i, j: (slab_j(i, j), slab_i(i))),
    ]
    out_shape = [jax.ShapeDtypeStruct((nf, d, 2 * tf), BF16), jax.ShapeDtypeStruct((d_ff, d), BF16)]
    return in_specs, out_specs, out_shape


def _cast_ffn_weights(w_gu, w_down, layer, half):
    d, d_ff = w_gu.shape[2], w_down.shape[2]
    tf = _ffn_tf(d_ff)
    cr = _cast_tiles(d, MAX_CAST_TILES)
    in_specs, out_specs, out_shape = _cast_specs(d, d_ff, tf, cr, layer, half)
    return pl.pallas_call(
        _cast_kernel,
        grid=(cr, d_ff // tf),
        in_specs=in_specs,
        out_specs=out_specs,
        out_shape=out_shape,
        compiler_params=_params("parallel", "parallel"),
        name="cast_w",
    )(w_gu, w_gu, w_down)


def _ffn(x, g, w_gu, w_down, nxt=None):
    n, d = x.shape
    d_ff = w_down.shape[0]
    tm = _pick(n, (768, 512, 256, 128))
    tf = _ffn_tf(d_ff)
    ni, nf = n // tm, d_ff // tf
    once = pl.Buffered(1)
    in_specs = [
        pl.BlockSpec((tm, d), lambda i, j: (i, 0), pipeline_mode=once),
        pl.BlockSpec((1, d), lambda i, j: (0, 0)),
        pl.BlockSpec((None, d, 2 * tf), lambda i, j: (j, 0, 0)),
        pl.BlockSpec((tf, d), lambda i, j: (j, 0)),
    ]
    out_specs = [pl.BlockSpec((tm, d), lambda i, j: (i, 0), pipeline_mode=once)]
    out_shape = [jax.ShapeDtypeStruct((n, d), F32)]
    args = [x, g.reshape(1, d), w_gu, w_down]
    cr = 0
    if nxt is not None:
        next_gu, next_dn, layer, half = nxt
        cr = _cast_tiles(d, ni)
        c_in, c_out, c_shape = _cast_specs(d, d_ff, tf, cr, layer, half)
        in_specs += c_in
        out_specs += c_out
        out_shape += c_shape
        args += [next_gu, next_gu, next_dn]
    out = pl.pallas_call(
        functools.partial(_ffn_kernel, n_col=_pick(d, (512, 256, 128)), cast_tiles=cr),
        grid=(ni, nf),
        in_specs=in_specs,
        out_specs=out_specs,
        out_shape=out_shape,
        scratch_shapes=[pltpu.VMEM((tm, d), BF16)],
        compiler_params=_params("arbitrary", "arbitrary", vmem=FFN_VMEM_LIMIT),
        name="ffn",
    )(*args)
    return out if nxt is not None else out[0]


def _qkv_kernel(*refs, aliased):
    x_ref, g_ref, w_ref = refs[:3]
    ob_ref, of_ref, h_ref = refs[4:] if aliased else refs[3:]
    j = pl.program_id(1)

    @pl.when(j == 0)
    def _():
        h_ref[...] = _rms(x_ref[...], g_ref[...]).astype(BF16)

    y = _dot(h_ref[...], w_ref[...])
    ob_ref[...] = y.astype(BF16)
    of_ref[...] = y.reshape(of_ref.shape)


def _qkv(x, g, w, q_cols, row0, rows, qkv_prev=None):
    n, d = x.shape
    nout = w.shape[1]
    tm = _pick(rows, (512, 256, 128))
    tn = _pick(math.gcd(q_cols, nout), (1024, 512, 256, 128))
    qb = q_cols // tn
    assert row0 % tm == 0
    off = row0 // tm
    in_specs = [
        pl.BlockSpec((tm, d), lambda i, j: (i + off, 0)),
        pl.BlockSpec((1, d), lambda i, j: (0, 0)),
        pl.BlockSpec((d, tn), lambda i, j: (0, j)),
    ]
    args = [x, g.reshape(1, d), w]
    if qkv_prev is not None:
        in_specs.append(pl.BlockSpec(memory_space=pl.ANY))
        args.append(qkv_prev)
    return pl.pallas_call(
        functools.partial(_qkv_kernel, aliased=qkv_prev is not None),
        grid=(rows // tm, nout // tn),
        in_specs=in_specs,
        out_specs=[
            pl.BlockSpec((tm, tn), lambda i, j: (i + off, j)),
            pl.BlockSpec((tm, tn // HEAD_DIM, HEAD_DIM), lambda i, j: (i, jnp.maximum(j - qb, 0), 0)),
        ],
        out_shape=[
            jax.ShapeDtypeStruct((n, nout), BF16),
            jax.ShapeDtypeStruct((rows, (nout - q_cols) // HEAD_DIM, HEAD_DIM), F32),
        ],
        input_output_aliases={3: 0} if qkv_prev is not None else {},
        scratch_shapes=[pltpu.VMEM((tm, d), BF16)],
        compiler_params=_params("parallel", "arbitrary"),
        name="qkv",
    )(*args)


def _mm_res_kernel(a_ref, w_ref, r_ref, o_ref):
    o_ref[...] = r_ref[...] + _dot(a_ref[...], w_ref[...])


def _mm_res(a, w, res):
    n, k = a.shape
    nout = w.shape[1]
    tm = _pick(n, (512, 256, 128))
    tn = _pick(nout, (1024, 512, 256, 128))
    return pl.pallas_call(
        _mm_res_kernel,
        grid=(n // tm, nout // tn),
        in_specs=[
            pl.BlockSpec((tm, k), lambda i, j: (i, 0)),
            pl.BlockSpec((k, tn), lambda i, j: (0, j)),
            pl.BlockSpec((tm, tn), lambda i, j: (i, j)),
        ],
        out_specs=pl.BlockSpec((tm, tn), lambda i, j: (i, j)),
        out_shape=jax.ShapeDtypeStruct((n, nout), F32),
        compiler_params=_params("parallel", "parallel"),
        name="mm_res",
    )(a, w, res)


def _final_norm_kernel(x_ref, g_ref, o_ref):
    o_ref[...] = _rms(x_ref[...], g_ref[...])


def _final_norm(x, g, row0, rows):
    d = x.shape[1]
    tm = _pick(rows, (512, 256, 128))
    assert row0 % tm == 0
    off = row0 // tm
    return pl.pallas_call(
        _final_norm_kernel,
        grid=(rows // tm,),
        in_specs=[
            pl.BlockSpec((tm, d), lambda i: (i + off, 0)),
            pl.BlockSpec((1, d), lambda i: (0, 0)),
        ],
        out_specs=pl.BlockSpec((tm, d), lambda i: (i, 0)),
        out_shape=jax.ShapeDtypeStruct((rows, d), F32),
        compiler_params=_params("parallel"),
        name="final_norm",
    )(x, g.reshape(1, d))


def _rope_group(r, cosz, sinz):
    return r * cosz + pltpu.roll(r, ROPE_DIM, axis=1) * sinz


def _mla_down_kernel(x_ref, g_ref, w_ref, gq_ref, gkv_ref, cosz_ref, sinz_ref,
                     qn_ref, ckv_ref, ckvb_ref, kr_ref, krb_ref, *, q_lora, kv_lora):
    h = _rms(x_ref[...], g_ref[...]).astype(BF16)
    y = _dot(h, w_ref[...])
    qn_ref[...] = _rms(y[:, :q_lora], gq_ref[...]).astype(BF16)
    ckv = _rms(y[:, q_lora:q_lora + kv_lora], gkv_ref[...])
    ckv_ref[...] = ckv
    ckvb_ref[...] = ckv.astype(BF16)
    r0 = q_lora + kv_lora
    kr = _rope_group(y[:, r0:r0 + 2 * ROPE_DIM], cosz_ref[...], sinz_ref[...])
    kr_ref[...] = kr[:, :ROPE_DIM]
    krb_ref[...] = kr.astype(BF16)


def _mla_down(x, g, w_ext, gq, gkv, cosz, sinz, q_lora, kv_lora):
    n, d = x.shape
    wn = w_ext.shape[1]
    tm = _pick(n, (256, 128))
    row = lambda i: (i, 0)
    fix = lambda i: (0, 0)
    return pl.pallas_call(
        functools.partial(_mla_down_kernel, q_lora=q_lora, kv_lora=kv_lora),
        grid=(n // tm,),
        in_specs=[
            pl.BlockSpec((tm, d), row),
            pl.BlockSpec((1, d), fix),
            pl.BlockSpec((d, wn), fix),
            pl.BlockSpec((1, q_lora), fix),
            pl.BlockSpec((1, kv_lora), fix),
            pl.BlockSpec((tm, LANES), row),
            pl.BlockSpec((tm, LANES), row),
        ],
        out_specs=[
            pl.BlockSpec((tm, q_lora), row),
            pl.BlockSpec((tm, kv_lora), row),
            pl.BlockSpec((tm, kv_lora), row),
            pl.BlockSpec((tm, ROPE_DIM), row),
            pl.BlockSpec((tm, 2 * ROPE_DIM), row),
        ],
        out_shape=[
            jax.ShapeDtypeStruct((n, q_lora), BF16),
            jax.ShapeDtypeStruct((n, kv_lora), F32),
            jax.ShapeDtypeStruct((n, kv_lora), BF16),
            jax.ShapeDtypeStruct((n, ROPE_DIM), F32),
            jax.ShapeDtypeStruct((n, 2 * ROPE_DIM), BF16),
        ],
        compiler_params=_params("parallel"),
        name="mla_down",
    )(x, g.reshape(1, d), w_ext, gq.reshape(1, -1), gkv.reshape(1, -1), cosz, sinz)


def _mla_q_kernel(a_ref, w_ref, cosz_ref, sinz_ref, o_ref):
    y = _dot(a_ref[...], w_ref[...])
    cosz = cosz_ref[...]
    sinz = sinz_ref[...]
    for c in range(0, y.shape[1], 2 * HEAD_DIM):
        rot = _rope_group(y[:, c + HEAD_DIM:c + 2 * HEAD_DIM], cosz, sinz)
        o_ref[:, c:c + HEAD_DIM] = y[:, c:c + HEAD_DIM].astype(BF16)
        o_ref[:, c + HEAD_DIM:c + 2 * HEAD_DIM] = rot.astype(BF16)


def _mla_q(qn, w_ext, cosz, sinz):
    n, k = qn.shape
    nout = w_ext.shape[1]
    tm = _pick(n, (512, 256, 128))
    tn = _pick(nout, (2048, 1024, 512, 256))
    return pl.pallas_call(
        _mla_q_kernel,
        grid=(n // tm, nout // tn),
        in_specs=[
            pl.BlockSpec((tm, k), lambda i, j: (i, 0)),
            pl.BlockSpec((k, tn), lambda i, j: (0, j)),
            pl.BlockSpec((tm, LANES), lambda i, j: (i, 0)),
            pl.BlockSpec((tm, LANES), lambda i, j: (i, 0)),
        ],
        out_specs=pl.BlockSpec((tm, tn), lambda i, j: (i, j)),
        out_shape=jax.ShapeDtypeStruct((n, nout), BF16),
        compiler_params=_params("parallel", "parallel"),
        name="mla_q",
    )(qn, w_ext, cosz, sinz)


def _mla_kv_kernel(c_ref, wk_ref, wv_ref, kr_ref, k_ref, v_ref):
    c = c_ref[...].astype(BF16)
    yk = _dot(c, wk_ref[...]).astype(BF16)
    v_ref[...] = _dot(c, wv_ref[...]).astype(BF16)
    kr = kr_ref[...].astype(BF16)
    for hh in range(yk.shape[1] // HEAD_DIM):
        k_ref[:, 2 * hh * HEAD_DIM:(2 * hh + 1) * HEAD_DIM] = yk[:, hh * HEAD_DIM:(hh + 1) * HEAD_DIM]
        k_ref[:, (2 * hh + 1) * HEAD_DIM:(2 * hh + 2) * HEAD_DIM] = kr


def _mla_kv(c, w_uk, w_uv, kr):
    n, k = c.shape
    hd = w_uk.shape[1]
    tm = _pick(n, (512, 256, 128))
    tn = _pick(hd, (2048, 1024, 512, 256, 128))
    return pl.pallas_call(
        _mla_kv_kernel,
        grid=(n // tm, hd // tn),
        in_specs=[
            pl.BlockSpec((tm, k), lambda i, j: (i, 0)),
            pl.BlockSpec((k, tn), lambda i, j: (0, j)),
            pl.BlockSpec((k, tn), lambda i, j: (0, j)),
            pl.BlockSpec((tm, LANES), lambda i, j: (i, 0)),
        ],
        out_specs=[
            pl.BlockSpec((tm, 2 * tn), lambda i, j: (i, j)),
            pl.BlockSpec((tm, tn), lambda i, j: (i, j)),
        ],
        out_shape=[
            jax.ShapeDtypeStruct((n, 2 * hd), BF16),
            jax.ShapeDtypeStruct((n, hd), BF16),
        ],
        compiler_params=_params("parallel", "parallel"),
        name="mla_kv",
    )(c, w_uk, w_uv, kr)


def _softmax_pv(parts):
    m = parts[0][0].max(axis=-1, keepdims=True)
    for t, _ in parts[1:]:
        m = jnp.maximum(m, t.max(axis=-1, keepdims=True))
    l = None
    acc = None
    for t, v in parts:
        p = jnp.exp2(t - m)
        ps = p.sum(axis=-1, keepdims=True)
        pv = _dot(p.astype(BF16), v)
        l = ps if l is None else l + ps
        acc = pv if acc is None else acc + pv
    return acc / l


def _interleave(n_streams, n_items):
    return [(hh, i) for i in range(n_items) for hh in range(n_streams)]


def _mla_prompt_kernel(q_ref, k_ref, v_ref, o_ref, *, tq, scale2):
    t = q_ref.shape[0]
    r = lax.broadcasted_iota(jnp.int32, (tq, tq), 0)
    c = lax.broadcasted_iota(jnp.int32, (tq, tq), 1)
    diag = c < (r | (CHUNK - 1)) + 1

    def scores(item, _):
        hh, i = item
        qk = slice(hh * 2 * HEAD_DIM, (hh + 1) * 2 * HEAD_DIM)
        return _dot_nt(q_ref[i * tq:(i + 1) * tq, qk], k_ref[0:(i + 1) * tq, qk]) * scale2

    def finish(item, s):
        hh, i = item
        vo = slice(hh * HEAD_DIM, (hh + 1) * HEAD_DIM)
        lo = i * tq
        parts = [(s[:, :lo], v_ref[0:lo, vo])] if i else []
        parts.append((jnp.where(diag, s[:, lo:], NEG_INF), v_ref[lo:lo + tq, vo]))
        o_ref[lo:lo + tq, vo] = _softmax_pv(parts).astype(o_ref.dtype)

    _skewed(_interleave(o_ref.shape[1] // HEAD_DIM, t // tq), [scores, finish])


def _mla_prompt_attn(q, k, v, batch, t, heads, scale):
    tq = _pick(t, (256, 128))
    hps = HEADS_PER_STEP
    return pl.pallas_call(
        functools.partial(_mla_prompt_kernel, tq=tq, scale2=scale * LOG2E),
        grid=(batch, heads // hps),
        in_specs=[
            pl.BlockSpec((t, hps * 2 * HEAD_DIM), lambda b, h: (b, h)),
            pl.BlockSpec((t, hps * 2 * HEAD_DIM), lambda b, h: (b, h)),
            pl.BlockSpec((t, hps * HEAD_DIM), lambda b, h: (b, h)),
        ],
        out_specs=pl.BlockSpec((t, hps * HEAD_DIM), lambda b, h: (b, h)),
        out_shape=jax.ShapeDtypeStruct((q.shape[0], heads * HEAD_DIM), BF16),
        compiler_params=_params("parallel", "parallel"),
        name="mla_attn_p",
    )(q, k, v)


def _mla_sample_kernel(q_ref, kp_ref, vp_ref, kn_ref, vn_ref, o_in_ref, o_ref, *, scale2):
    del o_in_ref
    for hh in range(o_ref.shape[1] // HEAD_DIM):
        qk = slice(hh * 2 * HEAD_DIM, (hh + 1) * 2 * HEAD_DIM)
        vo = slice(hh * HEAD_DIM, (hh + 1) * HEAD_DIM)
        q = q_ref[:, qk]
        parts = [(_dot_nt(q, kp_ref[:, qk]) * scale2, vp_ref[:, vo]),
                 (_dot_nt(q, kn_ref[:, qk]) * scale2, vn_ref[:, vo])]
        o_ref[:, vo] = _softmax_pv(parts).astype(o_ref.dtype)


def _mla_sample_attn(q, k_past, v_past, k_all, v_all, o_prev, batch, t, past, heads, row0, scale):
    assert row0 % t == 0
    off = row0 // t
    hps = _pick(heads, (4, 2, 1))
    new = lambda width: pl.BlockSpec((t, hps * width), lambda b, h: (b + off, h))
    old = lambda width: pl.BlockSpec((past, hps * width), lambda b, h: (b, h))
    return pl.pallas_call(
        functools.partial(_mla_sample_kernel, scale2=scale * LOG2E),
        grid=(batch, heads // hps),
        in_specs=[new(2 * HEAD_DIM), old(2 * HEAD_DIM), old(HEAD_DIM), new(2 * HEAD_DIM), new(HEAD_DIM),
                  pl.BlockSpec(memory_space=pl.ANY)],
        out_specs=new(HEAD_DIM),
        out_shape=jax.ShapeDtypeStruct(o_prev.shape, o_prev.dtype),
        input_output_aliases={5: 0},
        compiler_params=_params("parallel", "parallel"),
        name="mla_attn_s",
    )(q, k_past, v_past, k_all, v_all, o_prev)


def _band_prompt_kernel(q_ref, k_ref, v_ref, bias_ref, o_ref, *, tq, scale2):
    t = q_ref.shape[0]
    band = bias_ref.shape[2]

    def scores(item, _):
        hh, i = item
        cols = slice(hh * HEAD_DIM, (hh + 1) * HEAD_DIM)
        ke = (i + 1) * tq
        ks = max(0, ke - band)
        s = _dot_nt(q_ref[i * tq:(i + 1) * tq, cols], k_ref[ks:ke, cols])
        return s * scale2 + bias_ref[hh, :, band - (ke - ks):]

    def finish(item, s):
        hh, i = item
        cols = slice(hh * HEAD_DIM, (hh + 1) * HEAD_DIM)
        ke = (i + 1) * tq
        ks = max(0, ke - band)
        o_ref[i * tq:(i + 1) * tq, cols] = _softmax_pv([(s, v_ref[ks:ke, cols])]).astype(o_ref.dtype)

    _skewed(_interleave(bias_ref.shape[0], t // tq), [scores, finish])


def _band_prompt_attn(qkv, bias2, batch, t, heads, scale):
    hps = HEADS_PER_STEP
    groups = heads // hps
    spec = lambda g0: pl.BlockSpec((t, hps * HEAD_DIM), lambda b, h: (b, h + g0))
    return pl.pallas_call(
        functools.partial(_band_prompt_kernel, tq=bias2.shape[1], scale2=scale * LOG2E),
        grid=(batch, groups),
        in_specs=[spec(0), spec(groups), spec(2 * groups),
                  pl.BlockSpec((hps,) + bias2.shape[1:], lambda b, h: (h, 0, 0))],
        out_specs=spec(0),
        out_shape=jax.ShapeDtypeStruct((qkv.shape[0], heads * HEAD_DIM), BF16),
        compiler_params=_params("parallel", "parallel"),
        name="band_attn_p",
    )(qkv, qkv, qkv, bias2)


def _band_sample_kernel(q_ref, kp_ref, vp_ref, kn_ref, vn_ref, bias_ref, o_in_ref, o_ref, *, scale2):
    del o_in_ref
    win = kp_ref.shape[0]
    for hh in range(kp_ref.shape[1]):
        cols = slice(hh * HEAD_DIM, (hh + 1) * HEAD_DIM)
        q = q_ref[:, cols]
        parts = [(_dot_nt(q, kp_ref[:, hh, :].astype(BF16)) * scale2 + bias_ref[hh, :, :win],
                  vp_ref[:, hh, :].astype(BF16)),
                 (_dot_nt(q, kn_ref[:, cols]) * scale2 + bias_ref[hh, :, win:], vn_ref[:, cols])]
        o_ref[:, cols] = _softmax_pv(parts).astype(o_ref.dtype)


def _cache_view(cache):
    return cache.reshape(cache.shape[:2] + (-1, HEAD_DIM))


def _cache_spec(cache, g0):
    return pl.BlockSpec((None, cache.shape[1], HEAD_GROUP, HEAD_DIM), lambda b, hg: (b, 0, hg + g0, 0))


def _band_sample_attn(qkv, cache, bias, o_prev, batch, t, heads, row0, scale):
    assert row0 % t == 0
    off = row0 // t
    groups = heads // HEAD_GROUP
    new = lambda g0: pl.BlockSpec((t, HEAD_GROUP * HEAD_DIM), lambda b, hg: (b + off, hg + g0))
    return pl.pallas_call(
        functools.partial(_band_sample_kernel, scale2=scale * LOG2E),
        grid=(batch, groups),
        in_specs=[new(0), _cache_spec(cache, 0), _cache_spec(cache, groups), new(groups), new(2 * groups),
                  pl.BlockSpec((HEAD_GROUP,) + bias.shape[1:], lambda b, hg: (hg, 0, 0)),
                  pl.BlockSpec(memory_space=pl.ANY)],
        out_specs=new(0),
        out_shape=jax.ShapeDtypeStruct(o_prev.shape, o_prev.dtype),
        input_output_aliases={6: 0},
        compiler_params=_params("parallel", "parallel"),
        name="band_attn_s",
    )(qkv, _cache_view(cache), _cache_view(cache), qkv, qkv, bias, o_prev)


def _sb_items(n_q, tq, tk, lead):
    items = []
    for i in range(n_q):
        blocks = [(0, i * tq, tq, True)]
        end = i * tq
        while end > 0:
            w = min(tk, end)
            blocks.append((0, end - w, w, False))
            end -= w
        end = lead
        while end > 0:
            w = min(tk, end)
            blocks.append((1, end - w, w, False))
            end -= w
        for n, (src, k0, w, own) in enumerate(blocks):
            items.append((i, src, k0, w, own, n == 0, n == len(blocks) - 1))
    return items


def _sb_body(streams, *, t, tq, tk, lead, scale2):
    tri_n = max(tq, tk)
    row = lax.broadcasted_iota(jnp.int32, (tri_n, tri_n), 0)
    col = lax.broadcasted_iota(jnp.int32, (tri_n, tri_n), 1)
    tri = jnp.where(row > col, 1.0, 0.0).astype(BF16)
    keep = (lax.broadcasted_iota(jnp.int32, (tq, tq), 1)
            < lax.broadcasted_iota(jnp.int32, (tq, tq), 0))
    state = {}

    def terms(item, _):
        sid, i, src, k0, w, own, _, _ = item
        q_of, srcs, _ = streams[sid]
        nz = _dot_nt(q_of(i), srcs[src][0](k0, w)) * (-scale2)
        neg_abs = lax.bitcast_convert_type(
            lax.bitcast_convert_type(nz, jnp.uint32) | jnp.uint32(0x80000000), F32)
        lf = jnp.minimum(nz, 0.0) - jnp.log(1.0 + jnp.exp2(neg_abs)) * LOG2E
        if own:
            lf = jnp.where(keep, lf, 0.0)
        hi = lf.astype(BF16)
        lo = (lf - hi.astype(F32)).astype(BF16)
        return nz, lf, (hi, lo), lf.sum(axis=-1, keepdims=True)

    def suffix(item, st):
        w = item[4]
        hi, lo = st[2]
        tri_w = tri[:w, :w]
        if w % LANES == 0:
            within = _dot(jnp.concatenate([hi, lo], axis=1), jnp.concatenate([tri_w, tri_w], axis=0))
        else:
            within = _dot(hi, tri_w) + _dot(lo, tri_w)
        return st + (within,)

    def weigh(item, st):
        sid, i, src, k0, w, own, first, last = item
        _, srcs, put = streams[sid]
        nz, lf, _, total, within = st
        if first:
            state[sid] = (jnp.zeros((tq, 1), F32), jnp.zeros((tq, HEAD_DIM), F32))
        carry, acc = state[sid]
        a = jnp.exp2((lf - nz) + (within + carry))
        if own:
            a = jnp.where(keep, a, 0.0)
        acc = acc + _dot(a.astype(BF16), srcs[src][1](k0, w))
        state[sid] = (carry + total, acc)
        if last:
            put(i, acc)

    items = _sb_items(t // tq, tq, tk, lead)
    _skewed([(sid,) + it for it in items for sid in range(len(streams))], [terms, suffix, weigh])


def _rows_of(ref, c0=0):
    return lambda k0, w: ref[k0:k0 + w, c0:c0 + HEAD_DIM]


def _cached_rows_of(ref, hh):
    return lambda k0, w: ref[k0:k0 + w, hh, :].astype(BF16)


def _sb_prompt_kernel(q_ref, k_ref, v_ref, o_ref, *, tq, **kw):
    def stream(c0):
        def put(i, acc):
            o_ref[i * tq:(i + 1) * tq, c0:c0 + HEAD_DIM] = acc.astype(o_ref.dtype)

        return (lambda i: q_ref[i * tq:(i + 1) * tq, c0:c0 + HEAD_DIM],
                [(_rows_of(k_ref, c0), _rows_of(v_ref, c0))], put)

    _sb_body([stream(c0) for c0 in range(0, o_ref.shape[1], HEAD_DIM)],
             t=q_ref.shape[0], tq=tq, lead=0, **kw)


def _sb_sample_kernel(q_ref, kn_ref, vn_ref, kp_ref, vp_ref, o_in_ref, o_ref, **kw):
    del o_in_ref
    t = q_ref.shape[0]

    def stream(hh):
        c0 = hh * HEAD_DIM

        def put(i, acc):
            o_ref[:, c0:c0 + HEAD_DIM] = acc.astype(o_ref.dtype)

        return (lambda i: q_ref[:, c0:c0 + HEAD_DIM],
                [(_rows_of(kn_ref, c0), _rows_of(vn_ref, c0)),
                 (_cached_rows_of(kp_ref, hh), _cached_rows_of(vp_ref, hh))], put)

    for hh in range(0, kp_ref.shape[1], HEADS_PER_STEP):
        _sb_body([stream(hh + n) for n in range(HEADS_PER_STEP)], t=t, tq=t, lead=kp_ref.shape[0], **kw)


def _sb_prompt_attn(qkv, batch, t, heads, scale):
    tq = _pick(t, (256, 128, 64))
    groups = heads // HEADS_PER_STEP
    spec = lambda g0: pl.BlockSpec((t, HEADS_PER_STEP * HEAD_DIM), lambda b, h: (b, h + g0))
    return pl.pallas_call(
        functools.partial(_sb_prompt_kernel, tq=tq, tk=MXU_DIM, scale2=scale * LOG2E),
        grid=(batch, groups),
        in_specs=[spec(0), spec(groups), spec(2 * groups)],
        out_specs=spec(0),
        out_shape=jax.ShapeDtypeStruct((qkv.shape[0], heads * HEAD_DIM), BF16),
        compiler_params=_params("parallel", "parallel"),
        name="sb_attn_p",
    )(qkv, qkv, qkv)


def _sb_sample_attn(qkv, cache, o_prev, batch, t, heads, row0, scale):
    assert row0 % t == 0
    off = row0 // t
    groups = heads // HEAD_GROUP
    new = lambda g0: pl.BlockSpec((t, HEAD_GROUP * HEAD_DIM), lambda b, hg: (b + off, hg + g0))
    return pl.pallas_call(
        functools.partial(_sb_sample_kernel, tk=MXU_DIM, scale2=scale * LOG2E),
        grid=(batch, groups),
        in_specs=[new(0), new(groups), new(2 * groups), _cache_spec(cache, 0), _cache_spec(cache, groups),
                  pl.BlockSpec(memory_space=pl.ANY)],
        out_specs=new(0),
        out_shape=jax.ShapeDtypeStruct(o_prev.shape, o_prev.dtype),
        input_output_aliases={5: 0},
        compiler_params=_params("parallel", "parallel"),
        name="sb_attn_s",
    )(qkv, qkv, qkv, _cache_view(cache), _cache_view(cache), o_prev)


def _rope_tables(pos):
    half = ROPE_DIM // 2
    inv_freq = ROPE_THETA ** (-jnp.arange(half, dtype=F32) / half)
    ang = pos.astype(F32)[:, None] * inv_freq[None, :]
    cos = jnp.cos(ang)
    sin = jnp.sin(ang)
    zero = jnp.zeros((pos.shape[0], LANES - ROPE_DIM), F32)
    return jnp.concatenate([cos, cos, zero], axis=1), jnp.concatenate([sin, sin, zero], axis=1)


def _rotate_cols(w):
    half = ROPE_DIM // 2
    return jnp.concatenate([-w[..., half:], w[..., :half]], axis=-1)


def _mla_layer(x, g, cosz, sinz, past_ckv, past_krope, w_down, g_q, w_uq, g_kv, w_uk, w_uv, w_o, dims):
    bp, tp, bs, ts = dims
    np_rows = bp * tp
    q_lora = g_q.shape[0]
    kv_lora = g_kv.shape[0]
    heads = w_uk.shape[1]
    past = past_ckv.shape[1]
    scale = (HEAD_DIM + ROPE_DIM) ** -0.5

    w_rope = w_down[:, q_lora + kv_lora:]
    w_down_ext = jnp.concatenate([w_down, _rotate_cols(w_rope)], axis=1).astype(BF16)
    wq = w_uq.reshape(q_lora, heads, HEAD_DIM + ROPE_DIM)
    w_uq_ext = jnp.concatenate([wq, _rotate_cols(wq[..., HEAD_DIM:])], axis=-1)
    w_uq_ext = w_uq_ext.reshape(q_lora, heads * 2 * HEAD_DIM).astype(BF16)
    w_uk2 = w_uk.reshape(kv_lora, heads * HEAD_DIM).astype(BF16)
    w_uv2 = w_uv.reshape(kv_lora, heads * HEAD_DIM).astype(BF16)

    qn, ckv, ckv_b, kr, kr_b = _mla_down(x, g, w_down_ext, g_q, g_kv, cosz, sinz, q_lora, kv_lora)
    q = _mla_q(qn, w_uq_ext, cosz, sinz)
    k_all, v_all = _mla_kv(ckv_b, w_uk2, w_uv2, kr_b)
    kr_past = jnp.pad(past_krope.reshape(bs * past, ROPE_DIM), ((0, 0), (0, LANES - ROPE_DIM)))
    k_past, v_past = _mla_kv(past_ckv.reshape(bs * past, kv_lora), w_uk2, w_uv2, kr_past)

    o = _mla_prompt_attn(q, k_all, v_all, bp, tp, heads, scale)
    o = _mla_sample_attn(q, k_past, v_past, k_all, v_all, o, bs, ts, past, heads, np_rows, scale)
    x = _mm_res(o, w_o.astype(BF16), x)
    outs = (ckv[:np_rows].reshape(bp, tp, kv_lora), kr[:np_rows].reshape(bp, tp, ROPE_DIM),
            ckv[np_rows:].reshape(bs, ts, kv_lora), kr[np_rows:].reshape(bs, ts, ROPE_DIM))
    return x, outs


def _band_bias(rel_bias, nch):
    band = LEFT_CTX + CHUNK
    far = band - 1 - REL_CLIP
    u = jnp.concatenate([jnp.repeat(rel_bias[:, 2 * REL_CLIP:], far, axis=1),
                         rel_bias[:, REL_CLIP - (CHUNK - 1):2 * REL_CLIP + 1][:, ::-1]], axis=1) * LOG2E
    bias = jnp.stack([u[:, CHUNK - 1 - q:CHUNK - 1 - q + band] for q in range(CHUNK)], axis=1).astype(F32)
    rows = [jnp.pad(bias, ((0, 0), (0, 0), (c * CHUNK, (nch - 1 - c) * CHUNK)), constant_values=NEG_INF)
            for c in range(nch)]
    return jnp.concatenate(rows, axis=1)


def _band_layer(x, g, cache, w_qkv, rel_bias, w_o, dims):
    bp, tp, bs, ts = dims
    np_rows = bp * tp
    heads = rel_bias.shape[0]
    hd = heads * HEAD_DIM
    win = cache.shape[1]
    scale = HEAD_DIM ** -0.5

    w = w_qkv.astype(BF16)
    qkv_b, kv_p = _qkv(x, g, w, hd, 0, np_rows)
    qkv_b, kv_s = _qkv(x, g, w, hd, np_rows, bs * ts, qkv_b)
    o = _band_prompt_attn(qkv_b, _band_bias(rel_bias, 2), bp, tp, heads, scale)
    o = _band_sample_attn(qkv_b, cache, _band_bias(rel_bias, 1), o, bs, ts, heads, np_rows, scale)
    x = _mm_res(o, w_o.astype(BF16), x)

    wp = min(LEFT_CTX, tp)
    st_p = kv_p.reshape(bp, tp, 2, heads, HEAD_DIM)[:, tp - wp:]
    new = kv_s.reshape(bs, ts, 2, heads, HEAD_DIM)
    st_s = jnp.concatenate([cache, new], axis=1)[:, ts:]
    return x, (st_p, st_s)


def _sb_layer(x, g, cache, w_qkv, w_o, dims):
    bp, tp, bs, ts = dims
    np_rows = bp * tp
    hd = w_o.shape[0]
    heads = hd // HEAD_DIM
    scale = HEAD_DIM ** -0.5

    w = w_qkv.astype(BF16)
    qkv_b, kv_p = _qkv(x, g, w, hd, 0, np_rows)
    qkv_b, kv_s = _qkv(x, g, w, hd, np_rows, bs * ts, qkv_b)
    o = _sb_prompt_attn(qkv_b, bp, tp, heads, scale)
    o = _sb_sample_attn(qkv_b, cache, o, bs, ts, heads, np_rows, scale)
    x = _mm_res(o, w_o.astype(BF16), x)
    st_p = kv_p.reshape(bp, tp, 2, heads, HEAD_DIM)
    st_s = kv_s.reshape(bs, ts, 2, heads, HEAD_DIM)
    return x, (st_p, st_s)


def kernel(x_prompt, x_sample, cache_mla_ckv, cache_mla_krope, cache_band_kv, cache_sb_kv, norm_g, final_norm_g, ffn_w_gu, ffn_w_down, mla_w_down, mla_q_norm_g, mla_kv_norm_g, mla_w_uq, mla_w_uk, mla_w_uv, mla_w_o, band_w_qkv, band_rel_bias, band_w_o, sb_w_qkv, sb_w_o):
    bp, tp, d = x_prompt.shape
    bs, ts, _ = x_sample.shape
    depth = norm_g.shape[0]
    past = cache_mla_ckv.shape[2]
    dims = (bp, tp, bs, ts)
    np_rows = bp * tp
    ns_rows = bs * ts

    x = jnp.concatenate([x_prompt.reshape(np_rows, d), x_sample.reshape(ns_rows, d)], axis=0)
    w_gu, w_dn = _cast_ffn_weights(ffn_w_gu, ffn_w_down, 0, 0)

    def ffn(x, g, w_gu, w_dn, k):
        if k + 1 == 2 * depth:
            return _ffn(x, g, w_gu, w_dn), None, None
        return _ffn(x, g, w_gu, w_dn, (ffn_w_gu, ffn_w_down, (k + 1) // 2, (k + 1) % 2))

    pos = jnp.concatenate([jnp.tile(jnp.arange(tp), bp), jnp.tile(past + jnp.arange(ts), bs)])
    cosz, sinz = _rope_tables(pos)

    ckv_p, krope_p, band_p, sb_p = [], [], [], []
    ckv_s, krope_s, band_s, sb_s = [], [], [], []
    for i in range(depth):
        kind, j = i % 3, i // 3
        x, w_gu, w_dn = ffn(x, norm_g[i, 0], w_gu, w_dn, 2 * i)
        if kind == 0:
            x, (c_p, k_p, c_s, k_s) = _mla_layer(
                x, norm_g[i, 1], cosz, sinz, cache_mla_ckv[j], cache_mla_krope[j],
                mla_w_down[j], mla_q_norm_g[j], mla_w_uq[j], mla_kv_norm_g[j], mla_w_uk[j], mla_w_uv[j],
                mla_w_o[j], dims)
            ckv_p.append(c_p); krope_p.append(k_p); ckv_s.append(c_s); krope_s.append(k_s)
        elif kind == 1:
            x, (st_p, st_s) = _band_layer(x, norm_g[i, 1], cache_band_kv[j], band_w_qkv[j], band_rel_bias[j],
                                          band_w_o[j], dims)
            band_p.append(st_p); band_s.append(st_s)
        else:
            x, (st_p, st_s) = _sb_layer(x, norm_g[i, 1], cache_sb_kv[j], sb_w_qkv[j], sb_w_o[j], dims)
            sb_p.append(st_p); sb_s.append(st_s)
        x, w_gu, w_dn = ffn(x, norm_g[i, 2], w_gu, w_dn, 2 * i + 1)

    y_prompt = _final_norm(x, final_norm_g, 0, np_rows).reshape(bp, tp, d)
    y_sample = _final_norm(x, final_norm_g, np_rows, ns_rows).reshape(bs, ts, d)
    return (y_prompt, y_sample,
            jnp.stack(ckv_p), jnp.stack(krope_p), jnp.stack(band_p), jnp.stack(sb_p),
            jnp.stack(ckv_s), jnp.stack(krope_s), jnp.stack(band_s), jnp.stack(sb_s))
```
